```python
import math
import jax, jax.numpy as jnp
from jax import lax
import numpy as np

D_MODEL = 1024
BATCH = 8
SEQ = 4096
DEPTH = 2
DEC_BATCH = 128
DEC_SEQ = 8
PAST_LEN = 16384
PAGE_SIZE = 128

WINDOW = 128
DH_A = 64
H_A = (D_MODEL // 2) // DH_A
KVH_A = 2
G_A = H_A // KVH_A
H_B = 4
DK_B = (D_MODEL // 2) // H_B
DV_B = DK_B
MLSTM_CHUNK = 128
DH_C = 64
H_C = D_MODEL // (2 * DH_C)
KVH_C = 2
G_C = H_C // KVH_C
QBLOCK = 128
MEM_LEN = 256
XH = 4
XDH = D_MODEL // XH
D_FF = -(-8 * D_MODEL // (3 * 256)) * 256
EPS = 1e-6

SPLIT_AB = (H_A * DH_A, KVH_A * DH_A, KVH_A * DH_A, H_B * DK_B, H_B * DK_B, H_B * DV_B, H_B * DV_B, H_B, H_B)
IN_AB = sum(SPLIT_AB)
SPLIT_C = (H_C * 2 * DH_C, KVH_C * 2 * DH_C, KVH_C * 2 * DH_C)
IN_C = sum(SPLIT_C)

kernel_name = 'hybrid_swa_mlstm_diffattn_decoder_step'


def rmsnorm(x, g):
    xf = x.astype(jnp.float32)
    y = xf * lax.rsqrt(jnp.mean(xf * xf, axis=-1, keepdims=True) + EPS)
    return (y * g.astype(jnp.float32)).astype(x.dtype)


def split_cols(z, sizes):
    return jnp.split(z, np.cumsum(sizes)[:-1].tolist(), axis=-1)


def alibi_slopes(n):
    return jnp.exp2(-8.0 * jnp.arange(1, n + 1, dtype=jnp.float32) / n)


def lambda_init(layer):
    return 0.8 - 0.6 * math.exp(-0.3 * layer)


def window_sink_attention(q, k, v, q_pos, k_pos, sinks):
    s = jnp.einsum('ntkgd,nskd->nkgts', q, k, preferred_element_type=jnp.float32) * (DH_A ** -0.5)
    dist = (q_pos[:, :, None] - k_pos[:, None, :])[:, None, None]
    valid = (dist >= 0) & (dist <= WINDOW) & (k_pos >= 0)[:, None, None, None, :]
    slopes = alibi_slopes(H_A).reshape(KVH_A, G_A)[None, :, :, None, None]
    logits = jnp.where(valid, s - slopes * dist, -jnp.inf)
    sink = sinks.astype(jnp.float32).reshape(1, KVH_A, G_A, 1, 1)
    m = jnp.maximum(jnp.max(logits, axis=-1, keepdims=True), sink)
    e = jnp.exp(logits - m)
    p = e / (jnp.sum(e, axis=-1, keepdims=True) + jnp.exp(sink - m))
    return jnp.einsum('nkgts,nskd->ntkgd', p.astype(v.dtype), v)


def mlstm_chunkwise(q, k, v, i_pre, f_pre, C0, n0, m0):
    N, L = q.shape[:2]
    c = MLSTM_CHUNK if L % MLSTM_CHUNK == 0 else L
    f32 = jnp.float32

    def chunks(a):
        a = a.astype(f32)
        return jnp.moveaxis(a.reshape(N, L // c, c, *a.shape[2:]), 1, 0)

    causal = jnp.tril(jnp.ones((c, c), dtype=bool))[None, :, :, None]

    def step(carry, xs):
        C, n, m = carry
        qc, kc, vc, ic, fc = xs
        b = jnp.cumsum(jax.nn.log_sigmoid(fc), axis=1)
        d = jnp.where(causal, b[:, :, None] - b[:, None, :] + ic[:, None, :], -jnp.inf)
        inter = b + m[:, None]
        mt = jnp.maximum(inter, jnp.max(d, axis=2))
        sc = jnp.einsum('nthd,nshd->ntsh', qc, kc) * jnp.exp(d - mt[:, :, None])
        g = jnp.exp(inter - mt)
        num = jnp.einsum('ntsh,nshv->nthv', sc, vc) + g[..., None] * jnp.einsum('nthd,nhdv->nthv', qc, C)
        den = jnp.sum(sc, axis=2) + g * jnp.einsum('nthd,nhd->nth', qc, n)
        h = num / jnp.maximum(jnp.abs(den), jnp.exp(-mt))[..., None]
        m_new = mt[:, -1]
        w = jnp.exp(b[:, -1:] - b + ic - m_new[:, None])
        decay = jnp.exp(b[:, -1] + m - m_new)
        C_new = decay[..., None, None] * C + jnp.einsum('nsh,nshd,nshv->nhdv', w, kc, vc)
        n_new = decay[..., None] * n + jnp.einsum('nsh,nshd->nhd', w, kc)
        return (C_new, n_new, m_new), h

    init = (C0.astype(f32), n0.astype(f32), m0.astype(f32))
    (C, n, m), h = lax.scan(step, init, (chunks(q), chunks(k), chunks(v), chunks(i_pre), chunks(f_pre)))
    h = jnp.moveaxis(h, 0, 1).reshape(N, L, H_B * DV_B)
    return h, C.astype(C0.dtype), n.astype(n0.dtype), m.astype(m0.dtype)


def ab_project(h, w_in, b_i, b_f):
    N, T = h.shape[:2]
    qa, ka, va, qb, kb, vb, ob, ib, fb = split_cols(h @ w_in, SPLIT_AB)
    qa = qa.reshape(N, T, KVH_A, G_A, DH_A)
    ka = ka.reshape(N, T, KVH_A, DH_A)
    va = va.reshape(N, T, KVH_A, DH_A)
    qb = qb.reshape(N, T, H_B, DK_B)
    kb = kb.reshape(N, T, H_B, DK_B) * (DK_B ** -0.5)
    vb = vb.reshape(N, T, H_B, DV_B)
    return qa, ka, va, qb, kb, vb, ob, ib + b_i, fb + b_f


def ab_merge(ya, hb, ob, w_out):
    N, T = ya.shape[:2]
    yb = (jax.nn.sigmoid(ob.astype(jnp.float32)) * hb).astype(ya.dtype)
    return jnp.concatenate([ya.reshape(N, T, H_A * DH_A), yb], axis=-1) @ w_out


def ab_mixer_prompt(h, w_in, b_i, b_f, sinks, w_out):
    B, S = h.shape[:2]
    qa, ka, va, qb, kb, vb, ob, ib, fb = ab_project(h, w_in, b_i, b_f)
    nb = S // WINDOW

    def band(a):
        a = a.reshape(B, nb, WINDOW, *a.shape[2:])
        prev = jnp.concatenate([jnp.zeros_like(a[:, :1]), a[:, :-1]], axis=1)
        return jnp.concatenate([prev, a], axis=2).reshape(B * nb, 2 * WINDOW, *a.shape[3:])

    start = jnp.arange(nb, dtype=jnp.int32)[:, None] * WINDOW
    q_pos = jnp.broadcast_to((start + jnp.arange(WINDOW, dtype=jnp.int32))[None], (B, nb, WINDOW)).reshape(B * nb, WINDOW)
    k_pos = jnp.broadcast_to((start - WINDOW + jnp.arange(2 * WINDOW, dtype=jnp.int32))[None], (B, nb, 2 * WINDOW)).reshape(B * nb, 2 * WINDOW)
    ya = window_sink_attention(qa.reshape(B * nb, WINDOW, KVH_A, G_A, DH_A), band(ka), band(va), q_pos, k_pos, sinks)
    ya = ya.reshape(B, S, H_A * DH_A)
    C0 = jnp.zeros((B, H_B, DK_B, DV_B), h.dtype)
    n0 = jnp.zeros((B, H_B, DK_B), h.dtype)
    m0 = jnp.zeros((B, H_B), h.dtype)
    hb, C, n, m = mlstm_chunkwise(qb, kb, vb, ib, fb, C0, n0, m0)
    return ab_merge(ya, hb, ob, w_out), ka[:, -WINDOW:], va[:, -WINDOW:], C, n, m


def ab_mixer_sample(h, k_buf, v_buf, C0, n0, m0, w_in, b_i, b_f, sinks, w_out):
    T = h.shape[1]
    W = k_buf.shape[1]
    qa, ka, va, qb, kb, vb, ob, ib, fb = ab_project(h, w_in, b_i, b_f)
    k_all = jnp.concatenate([k_buf.astype(ka.dtype), ka], axis=1)
    v_all = jnp.concatenate([v_buf.astype(va.dtype), va], axis=1)
    k_pos = (PAST_LEN - W + jnp.arange(W + T, dtype=jnp.int32))[None]
    q_pos = (PAST_LEN + jnp.arange(T, dtype=jnp.int32))[None]
    ya = window_sink_attention(qa, k_all, v_all, q_pos, k_pos, sinks)
    hb, C, n, m = mlstm_chunkwise(qb, kb, vb, ib, fb, C0, n0, m0)
    return ab_merge(ya, hb, ob, w_out), k_all[:, -W:], v_all[:, -W:], C, n, m


def diff_lambda(lq1, lk1, lq2, lk2, lam_init):
    f = lambda a: a.astype(jnp.float32)
    return jnp.exp(jnp.sum(f(lq1) * f(lk1))) - jnp.exp(jnp.sum(f(lq2) * f(lk2))) + lam_init


def diff_attention_core(q, k, v, q_pos, k_pos, lam):
    s = jnp.einsum('ntkgmd,nskmd->nkgmts', q, k, preferred_element_type=jnp.float32) * (DH_C ** -0.5)
    dist = (q_pos[:, :, None] - k_pos[:, None, :])[:, None, None, None]
    slopes = alibi_slopes(H_C).reshape(KVH_C, G_C)[None, :, :, None, None, None]
    logits = jnp.where(dist >= 0, s - slopes * dist, -jnp.inf)
    p = jax.nn.softmax(logits, axis=-1)
    a = p[:, :, :, 0] - lam * p[:, :, :, 1]
    return jnp.einsum('nkgts,nskd->ntkgd', a.astype(v.dtype), v)


def diff_head_out(o, gain, lam_init):
    N, T = o.shape[:2]
    return (rmsnorm(o, gain) * (1.0 - lam_init)).reshape(N, T, H_C * 2 * DH_C)


def c_project(h, w_in):
    N, T = h.shape[:2]
    q, k, v = split_cols(h @ w_in, SPLIT_C)
    return (q.reshape(N, T, KVH_C, G_C, 2, DH_C), k.reshape(N, T, KVH_C, 2, DH_C), v.reshape(N, T, KVH_C, 2 * DH_C))


def c_mixer_prompt(h, w_in, lam, lam_init, gain, w_out):
    B, S = h.shape[:2]
    q, k, v = c_project(h, w_in)
    nb = S // QBLOCK
    q_blocks = jnp.moveaxis(q.reshape(B, nb, QBLOCK, *q.shape[2:]), 1, 0)
    q_pos = jnp.arange(S, dtype=jnp.int32).reshape(nb, QBLOCK)
    k_pos = jnp.arange(S, dtype=jnp.int32)[None]
    o = lax.map(lambda a: diff_attention_core(a[0], k, v, a[1][None], k_pos, lam), (q_blocks, q_pos))
    o = jnp.moveaxis(o, 0, 1).reshape(B, S, KVH_C, G_C, 2 * DH_C)
    return diff_head_out(o, gain, lam_init) @ w_out, k, v


def c_mixer_sample(h, cache_k, cache_v, layer_idx, page_table, w_in, lam, lam_init, gain, w_out):
    T = h.shape[1]
    q, k, v = c_project(h, w_in)
    past = page_table.shape[1] * PAGE_SIZE
    k_pos = jnp.arange(past + T, dtype=jnp.int32)[None]
    q_pos = (past + jnp.arange(T, dtype=jnp.int32))[None]

    def one(a):
        qi, ki, vi, pt = a
        k_past = cache_k[layer_idx, pt].reshape(past, KVH_C, 2, DH_C).astype(ki.dtype)
        v_past = cache_v[layer_idx, pt].reshape(past, KVH_C, 2 * DH_C).astype(vi.dtype)
        k_all = jnp.concatenate([k_past, ki], axis=0)
        v_all = jnp.concatenate([v_past, vi], axis=0)
        return diff_attention_core(qi[None], k_all[None], v_all[None], q_pos, k_pos, lam)[0]

    o = lax.map(one, (q, k, v, page_table))
    return diff_head_out(o, gain, lam_init) @ w_out, k, v


def cross_attention(h, mk, mv, w_q, w_o):
    N, T = h.shape[:2]
    q = (h @ w_q).reshape(N, T, XH, XDH)
    s = jnp.einsum('nthd,nmhd->nhtm', q, mk, preferred_element_type=jnp.float32) * (XDH ** -0.5)
    p = jax.nn.softmax(s, axis=-1)
    o = jnp.einsum('nhtm,nmhd->nthd', p.astype(mv.dtype), mv).reshape(N, T, XH * XDH)
    return o @ w_o


def swiglu(h, wg, wu, wd):
    return (jax.nn.silu(h @ wg) * (h @ wu)) @ wd


def setup_inputs(seed: int = 0) -> dict:
    key = jax.random.key(seed)
    keys = iter(jax.random.split(key, 48))

    def nrm(shape, scale=1.0):
        return jax.random.normal(next(keys), shape, jnp.float32) * scale

    def gain(shape):
        return 1.0 + nrm(shape, 0.02)

    n_even = (DEPTH + 1) // 2
    n_odd = DEPTH // 2
    n_pages = PAST_LEN // PAGE_SIZE
    n_pool = (DEC_BATCH * n_pages * 5) // 4
    page_table = jax.random.permutation(next(keys), n_pool)[: DEC_BATCH * n_pages].reshape(DEC_BATCH, n_pages).astype(jnp.int32)
    mix_ab = H_A * DH_A + H_B * DV_B
    mix_c = H_C * 2 * DH_C
    return {
        'x_prompt': nrm((BATCH, SEQ, D_MODEL)),
        'x_sample': nrm((DEC_BATCH, DEC_SEQ, D_MODEL)),
        'mem_prompt': nrm((BATCH, MEM_LEN, D_MODEL)),
        'cache_swa_k': nrm((n_even, DEC_BATCH, WINDOW, KVH_A, DH_A)),
        'cache_swa_v': nrm((n_even, DEC_BATCH, WINDOW, KVH_A, DH_A)),
        'state_mlstm_C': nrm((n_even, DEC_BATCH, H_B, DK_B, DV_B), 0.5),
        'state_mlstm_n': nrm((n_even, DEC_BATCH, H_B, DK_B)),
        'state_mlstm_m': nrm((n_even, DEC_BATCH, H_B)),
        'cache_diff_k': nrm((n_odd, n_pool, PAGE_SIZE, KVH_C, 2, DH_C)),
        'cache_diff_v': nrm((n_odd, n_pool, PAGE_SIZE, KVH_C, 2 * DH_C)),
        'page_table': page_table,
        'cache_mem_k': nrm((DEPTH, DEC_BATCH, MEM_LEN, XH, XDH)),
        'cache_mem_v': nrm((DEPTH, DEC_BATCH, MEM_LEN, XH, XDH)),
        'norm_mix': gain((DEPTH, D_MODEL)),
        'norm_cross': gain((DEPTH, D_MODEL)),
        'norm_ffn': gain((DEPTH, D_MODEL)),
        'norm_final': gain((D_MODEL,)),
        'w_in_ab': nrm((n_even, D_MODEL, IN_AB), D_MODEL ** -0.5),
        'b_mlstm_i': nrm((n_even, H_B), 0.1),
        'b_mlstm_f': jnp.linspace(3.0, 6.0, H_B, dtype=jnp.float32)[None] + nrm((n_even, H_B), 0.01),
        'attn_sinks': nrm((n_even, H_A), 0.5),
        'w_out_ab': nrm((n_even, mix_ab, D_MODEL), mix_ab ** -0.5),
        'w_in_c': nrm((n_odd, D_MODEL, IN_C), D_MODEL ** -0.5),
        'lambda_q1': nrm((n_odd, DH_C), 0.1),
        'lambda_k1': nrm((n_odd, DH_C), 0.1),
        'lambda_q2': nrm((n_odd, DH_C), 0.1),
        'lambda_k2': nrm((n_odd, DH_C), 0.1),
        'diff_norm': gain((n_odd, 2 * DH_C)),
        'w_out_c': nrm((n_odd, mix_c, D_MODEL), mix_c ** -0.5),
        'w_xq': nrm((DEPTH, D_MODEL, XH * XDH), D_MODEL ** -0.5),
        'w_xk': nrm((DEPTH, D_MODEL, XH * XDH), D_MODEL ** -0.5),
        'w_xv': nrm((DEPTH, D_MODEL, XH * XDH), D_MODEL ** -0.5),
        'w_xo': nrm((DEPTH, XH * XDH, D_MODEL), (XH * XDH) ** -0.5),
        'w_gate': nrm((DEPTH, D_MODEL, D_FF), D_MODEL ** -0.5),
        'w_up': nrm((DEPTH, D_MODEL, D_FF), D_MODEL ** -0.5),
        'w_down': nrm((DEPTH, D_FF, D_MODEL), D_FF ** -0.5),
    }


def reference(x_prompt, x_sample, mem_prompt, cache_swa_k, cache_swa_v, state_mlstm_C, state_mlstm_n, state_mlstm_m, cache_diff_k, cache_diff_v, page_table, cache_mem_k, cache_mem_v, norm_mix, norm_cross, norm_ffn, norm_final, w_in_ab, b_mlstm_i, b_mlstm_f, attn_sinks, w_out_ab, w_in_c, lambda_q1, lambda_k1, lambda_q2, lambda_k2, diff_norm, w_out_c, w_xq, w_xk, w_xv, w_xo, w_gate, w_up, w_down):
    xp, xs = x_prompt, x_sample
    B = xp.shape[0]
    M = mem_prompt.shape[1]
    swa_kp, swa_vp, swa_ks, swa_vs = [], [], [], []
    mC_p, mn_p, mm_p, mC_s, mn_s, mm_s = [], [], [], [], [], []
    dk_p, dv_p, dk_s, dv_s = [], [], [], []
    mk_p, mv_p = [], []
    for layer in range(DEPTH):
        hp = rmsnorm(xp, norm_mix[layer])
        hs = rmsnorm(xs, norm_mix[layer])
        if layer % 2 == 0:
            e = layer // 2
            wts = (w_in_ab[e], b_mlstm_i[e], b_mlstm_f[e], attn_sinks[e], w_out_ab[e])
            op, kb, vb, C, n, m = ab_mixer_prompt(hp, *wts)
            swa_kp.append(kb); swa_vp.append(vb); mC_p.append(C); mn_p.append(n); mm_p.append(m)
            os_, kb, vb, C, n, m = ab_mixer_sample(hs, cache_swa_k[e], cache_swa_v[e], state_mlstm_C[e], state_mlstm_n[e], state_mlstm_m[e], *wts)
            swa_ks.append(kb); swa_vs.append(vb); mC_s.append(C); mn_s.append(n); mm_s.append(m)
        else:
            c = layer // 2
            lam_init = lambda_init(layer)
            lam = diff_lambda(lambda_q1[c], lambda_k1[c], lambda_q2[c], lambda_k2[c], lam_init)
            wts = (lam, lam_init, diff_norm[c], w_out_c[c])
            op, k, v = c_mixer_prompt(hp, w_in_c[c], *wts)
            dk_p.append(k); dv_p.append(v)
            os_, k, v = c_mixer_sample(hs, cache_diff_k, cache_diff_v, c, page_table, w_in_c[c], *wts)
            dk_s.append(k); dv_s.append(v)
        xp = xp + op
        xs = xs + os_
        mk = (mem_prompt @ w_xk[layer]).reshape(B, M, XH, XDH)
        mv = (mem_prompt @ w_xv[layer]).reshape(B, M, XH, XDH)
        mk_p.append(mk); mv_p.append(mv)
        xp = xp + cross_attention(rmsnorm(xp, norm_cross[layer]), mk, mv, w_xq[layer], w_xo[layer])
        xs = xs + cross_attention(rmsnorm(xs, norm_cross[layer]), cache_mem_k[layer].astype(xs.dtype), cache_mem_v[layer].astype(xs.dtype), w_xq[layer], w_xo[layer])
        xp = xp + swiglu(rmsnorm(xp, norm_ffn[layer]), w_gate[layer], w_up[layer], w_down[layer])
        xs = xs + swiglu(rmsnorm(xs, norm_ffn[layer]), w_gate[layer], w_up[layer], w_down[layer])
    y_prompt = rmsnorm(xp, norm_final)
    y_sample = rmsnorm(xs, norm_final)
    st = jnp.stack
    return (y_prompt, y_sample, st(swa_kp), st(swa_vp), st(swa_ks), st(swa_vs), st(mC_p), st(mn_p), st(mm_p), st(mC_s), st(mn_s), st(mm_s), st(dk_p), st(dv_p), st(dk_s), st(dv_s), st(mk_p), st(mv_p))
```

```python
import functools
import math
from typing import NamedTuple

import jax
import jax.numpy as jnp
from jax import lax
from jax.experimental import pallas as pl
from jax.experimental.pallas import tpu as pltpu

F32 = jnp.float32
BF16 = jnp.bfloat16
EPS = 1e-6

WINDOW = 128
DH_A, KVH_A, G_A = 64, 2, 4
H_A = KVH_A * G_A
H_B, DK_B = 4, 128
MLSTM_CHUNK = 128
DH_C, KVH_C, G_C = 64, 2, 4
H_C = KVH_C * G_C
XH = 4
PAGE = 128

VMEM_LIMIT = 56 * 1024 * 1024


def _alibi_slope(h, n):
    return 2.0 ** (-8.0 * (h + 1) / n)


def _cparams(*sem):
    return pltpu.CompilerParams(dimension_semantics=sem, vmem_limit_bytes=VMEM_LIMIT)


def _resident(shape):
    return pl.BlockSpec(shape, lambda *_: (0,) * len(shape), pipeline_mode=pl.Buffered(1))


def _dot(a, b):
    return jnp.dot(a, b, preferred_element_type=F32)


def _dot_nt(a, b):
    return lax.dot_general(a, b, (((1,), (1,)), ((), ())), preferred_element_type=F32)


def _dot_tn(a, b):
    return lax.dot_general(a, b, (((0,), (0,)), ((), ())), preferred_element_type=F32)


def _rms(x, g):
    return x * lax.rsqrt(jnp.mean(x * x, axis=-1, keepdims=True) + EPS) * g


class Seg(NamedTuple):
    start: int
    width: int
    dtype: object
    scale: float = 1.0
    bias: bool = False


def _norm_matmul_kernel(x_ref, g_ref, w_ref, b_ref, *out_refs, segs, norm):
    x = x_ref[...]
    h = (_rms(x, g_ref[...]) if norm else x).astype(BF16)
    for o_ref, s in zip(out_refs, segs):
        z = _dot(h, w_ref[:, s.start:s.start + s.width])
        if s.bias:
            z = z + b_ref[:, s.start:s.start + s.width]
        if s.scale != 1.0:
            z = z * s.scale
        o_ref[...] = z.astype(o_ref.dtype)


def norm_matmul(x, gain, w, bias, segs, *, tm, norm=True):
    T, D = x.shape
    N = w.shape[1]
    assert T % tm == 0
    return pl.pallas_call(
        functools.partial(_norm_matmul_kernel, segs=tuple(segs), norm=norm),
        grid=(T // tm,),
        in_specs=[pl.BlockSpec((tm, D), lambda i: (i, 0)), _resident((1, D)), _resident((D, N)), _resident((1, N))],
        out_specs=[pl.BlockSpec((tm, s.width), lambda i: (i, 0)) for s in segs],
        out_shape=[jax.ShapeDtypeStruct((T, s.width), s.dtype) for s in segs],
        compiler_params=_cparams("parallel"),
    )(x, gain.reshape(1, D), w, bias)


def _matmul_residual_kernel(*refs, n_in):
    a_refs, w_refs, x_ref, o_ref = refs[:n_in], refs[n_in:2 * n_in], refs[2 * n_in], refs[2 * n_in + 1]
    acc = x_ref[...]
    for a_ref, w_ref in zip(a_refs, w_refs):
        acc = acc + _dot(a_ref[...].astype(BF16), w_ref[...])
    o_ref[...] = acc


def matmul_residual(acts, ws, x, *, tm):
    T, D = x.shape
    n_in = len(acts)
    return pl.pallas_call(
        functools.partial(_matmul_residual_kernel, n_in=n_in),
        grid=(T // tm,),
        in_specs=([pl.BlockSpec((tm, a.shape[1]), lambda i: (i, 0)) for a in acts]
                  + [_resident(w.shape) for w in ws] + [pl.BlockSpec((tm, D), lambda i: (i, 0))]),
        out_specs=pl.BlockSpec((tm, D), lambda i: (i, 0)),
        out_shape=jax.ShapeDtypeStruct((T, D), F32),
        compiler_params=_cparams("parallel"),
    )(*acts, *ws, x)


def _ffn_kernel(x_ref, g_ref, wg_ref, wu_ref, wd_ref, gf_ref, o_ref, *, ff_chunk, final_norm):
    x = x_ref[...]
    h = _rms(x, g_ref[...]).astype(BF16)
    acc = x
    for c in range(0, wg_ref.shape[1], ff_chunk):
        gate = _dot(h, wg_ref[:, c:c + ff_chunk])
        up = _dot(h, wu_ref[:, c:c + ff_chunk])
        a = gate * jax.nn.sigmoid(gate) * up
        acc = acc + _dot(a.astype(BF16), wd_ref[c:c + ff_chunk, :])
    o_ref[...] = _rms(acc, gf_ref[...]) if final_norm else acc


def ffn(x, gain, wg, wu, wd, gain_final, *, tm, final_norm, ff_chunk=256):
    T, D = x.shape
    F = wg.shape[1]
    assert F % ff_chunk == 0
    return pl.pallas_call(
        functools.partial(_ffn_kernel, ff_chunk=ff_chunk, final_norm=final_norm),
        grid=(T // tm,),
        in_specs=[pl.BlockSpec((tm, D), lambda i: (i, 0)), _resident((1, D)), _resident((D, F)), _resident((D, F)),
                  _resident((F, D)), _resident((1, D))],
        out_specs=pl.BlockSpec((tm, D), lambda i: (i, 0)),
        out_shape=jax.ShapeDtypeStruct((T, D), F32),
        compiler_params=_cparams("parallel"),
    )(x, gain.reshape(1, D), wg, wu, wd, gain_final.reshape(1, D))


def _sink_softmax(pieces, sink):
    m = sink
    for lg in pieces:
        m = jnp.maximum(m, jnp.max(lg, axis=-1, keepdims=True))
    es = [jnp.exp(lg - m) for lg in pieces]
    den = jnp.exp(sink - m)
    for e in es:
        den = den + jnp.sum(e, axis=-1, keepdims=True)
    inv = 1.0 / den
    return [e * inv for e in es]


def _swa_prompt_kernel(sink_ref, q_ref, kvp_ref, kvo_ref, o_ref):
    W = WINDOW
    j = pl.program_id(1)
    q = q_ref[...]
    kv = jnp.concatenate([kvp_ref[...], kvo_ref[...]], axis=0).astype(BF16)
    t = lax.broadcasted_iota(jnp.int32, (W, 2 * W), 0)
    s = lax.broadcasted_iota(jnp.int32, (W, 2 * W), 1)
    dist = t + W - s
    valid = (dist >= 0) & (dist <= W) & ((s >= W) | (j > 0))
    distf = dist.astype(F32)
    for kh in range(KVH_A):
        k = kv[:, kh * DH_A:(kh + 1) * DH_A]
        v = kv[:, (KVH_A + kh) * DH_A:(KVH_A + kh + 1) * DH_A]
        for g in range(G_A):
            h = kh * G_A + g
            sc = _dot_nt(q[:, h * DH_A:(h + 1) * DH_A], k)
            logits = jnp.where(valid, sc - _alibi_slope(h, H_A) * distf, -jnp.inf)
            (p,) = _sink_softmax([logits], sink_ref[0, h])
            o_ref[:, h * DH_A:(h + 1) * DH_A] = _dot(p.astype(BF16), v).astype(o_ref.dtype)


def swa_prompt(q, kv, sinks, B, S):
    W = WINDOW
    nb = S // W
    return pl.pallas_call(
        _swa_prompt_kernel,
        grid=(B, nb),
        in_specs=[pl.BlockSpec(memory_space=pltpu.SMEM),
                  pl.BlockSpec((W, q.shape[1]), lambda b, j: (b * nb + j, 0)),
                  pl.BlockSpec((W, kv.shape[1]), lambda b, j: (b * nb + jnp.maximum(j - 1, 0), 0)),
                  pl.BlockSpec((W, kv.shape[1]), lambda b, j: (b * nb + j, 0))],
        out_specs=pl.BlockSpec((W, q.shape[1]), lambda b, j: (b * nb + j, 0)),
        out_shape=jax.ShapeDtypeStruct(q.shape, BF16),
        compiler_params=_cparams("parallel", "parallel"),
    )(sinks.reshape(1, H_A), q, kv, kv)


def _swa_sample_kernel(sink_ref, q_ref, kvn_ref, kc_ref, vc_ref, o_ref, ko_ref, vo_ref, *, bb, T):
    W = WINDOW
    R = G_A * T
    row = lax.broadcasted_iota(jnp.int32, (R, 1), 0)
    t_row = row % T
    g_row = row // T
    s_c = lax.broadcasted_iota(jnp.int32, (R, W), 1)
    s_n = lax.broadcasted_iota(jnp.int32, (R, T), 1)
    dist_c = (t_row + W - s_c).astype(F32)
    dist_n = (t_row - s_n).astype(F32)
    valid_c = s_c >= t_row
    valid_n = s_n <= t_row
    for i in range(bb):
        q = q_ref[i * T:(i + 1) * T, :]
        kvn = kvn_ref[i * T:(i + 1) * T, :]
        kc = kc_ref[i]
        vc = vc_ref[i]
        nk = KVH_A * DH_A
        ko_ref[i, 0:W - T, :] = kc[T:, :]
        ko_ref[i, W - T:W, :] = kvn[:, :nk]
        vo_ref[i, 0:W - T, :] = vc[T:, :]
        vo_ref[i, W - T:W, :] = kvn[:, nk:]
        for kh in range(KVH_A):
            qs = jnp.concatenate([q[:, (kh * G_A + g) * DH_A:(kh * G_A + g + 1) * DH_A] for g in range(G_A)], axis=0)
            qs = qs.astype(BF16)
            slope = jnp.zeros((R, 1), F32)
            sink = jnp.zeros((R, 1), F32)
            for g in range(G_A):
                slope = jnp.where(g_row == g, _alibi_slope(kh * G_A + g, H_A), slope)
                sink = jnp.where(g_row == g, sink_ref[0, kh * G_A + g], sink)
            lc = _dot_nt(qs, kc[:, kh * DH_A:(kh + 1) * DH_A].astype(BF16))
            ln = _dot_nt(qs, kvn[:, kh * DH_A:(kh + 1) * DH_A].astype(BF16))
            lc = jnp.where(valid_c, lc - slope * dist_c, -jnp.inf)
            ln = jnp.where(valid_n, ln - slope * dist_n, -jnp.inf)
            pc, pn = _sink_softmax([lc, ln], sink)
            o = (_dot(pc.astype(BF16), vc[:, kh * DH_A:(kh + 1) * DH_A].astype(BF16))
                 + _dot(pn.astype(BF16), kvn[:, nk + kh * DH_A:nk + (kh + 1) * DH_A].astype(BF16)))
            for g in range(G_A):
                h = kh * G_A + g
                o_ref[i * T:(i + 1) * T, h * DH_A:(h + 1) * DH_A] = o[g * T:(g + 1) * T, :]


def swa_sample(q, kvn, k_cache, v_cache, sinks, N, T, *, bb=8):
    W = WINDOW
    nk = KVH_A * DH_A
    rows = lambda width: pl.BlockSpec((bb * T, width), lambda i: (i, 0))
    cache = pl.BlockSpec((bb, W, nk), lambda i: (i, 0, 0))
    return pl.pallas_call(
        functools.partial(_swa_sample_kernel, bb=bb, T=T),
        grid=(N // bb,),
        in_specs=[pl.BlockSpec(memory_space=pltpu.SMEM), rows(q.shape[1]), rows(kvn.shape[1]), cache, cache],
        out_specs=[rows(q.shape[1]), cache, cache],
        out_shape=[jax.ShapeDtypeStruct(q.shape, F32), jax.ShapeDtypeStruct(k_cache.shape, F32),
                   jax.ShapeDtypeStruct(v_cache.shape, F32)],
        compiler_params=_cparams("parallel"),
    )(sinks.reshape(1, H_A), q, kvn, k_cache, v_cache)


def _mlstm_head(q, k, v, i_col, f_col, C, n, m):
    c = q.shape[0]
    r = lax.broadcasted_iota(jnp.int32, (c, c), 0)
    s = lax.broadcasted_iota(jnp.int32, (c, c), 1)
    eye = r == s
    causal = s <= r
    lf = jax.nn.log_sigmoid(f_col)
    lf_row = jnp.sum(jnp.where(eye, lf, 0.0), axis=0, keepdims=True)
    i_row = jnp.sum(jnp.where(eye, i_col, 0.0), axis=0, keepdims=True)
    b_col = jnp.sum(jnp.where(causal, lf_row, 0.0), axis=1, keepdims=True)
    b_row = jnp.sum(jnp.where(r <= s, lf, 0.0), axis=0, keepdims=True)
    d = jnp.where(causal, b_col - b_row + i_row, -jnp.inf)
    inter = b_col + m
    mt = jnp.maximum(inter, jnp.max(d, axis=1, keepdims=True))
    qb, kb, vb = q.astype(BF16), k.astype(BF16), v.astype(BF16)
    sc = _dot_nt(qb, kb) * jnp.exp(d - mt)
    g = jnp.exp(inter - mt)
    num = _dot(sc.astype(BF16), vb) + g * _dot(qb, C.astype(BF16))
    den = jnp.sum(sc, axis=1, keepdims=True) + g * jnp.sum(q * n, axis=1, keepdims=True)
    h = num / jnp.maximum(jnp.abs(den), jnp.exp(-mt))
    m_new = mt[c - 1:c, :]
    b_last = b_col[c - 1:c, :]
    w = jnp.exp(b_last - b_col + i_col - m_new)
    decay = jnp.exp(b_last + m - m_new)
    kw = k * w
    C_new = decay * C + _dot_tn(kw.astype(BF16), vb)
    n_new = decay * n + jnp.sum(kw, axis=0, keepdims=True)
    return h, C_new, n_new, m_new


def _mlstm_kernel(q_ref, k_ref, v_ref, ob_ref, g_ref, C0_ref, n0_ref, m0_ref, y_ref, C_ref, n_ref, m_ref, *, bb, c):
    d = DK_B

    @pl.when(pl.program_id(1) == 0)
    def _():
        C_ref[...] = C0_ref[...]
        n_ref[...] = n0_ref[...]
        m_ref[...] = m0_ref[...]

    q = q_ref[...].astype(F32)
    k = k_ref[...].astype(F32)
    v = v_ref[...].astype(F32)
    gates = g_ref[...]
    for i in range(bb):
        rows = slice(i * c, (i + 1) * c)
        for h in range(H_B):
            cols = slice(h * d, (h + 1) * d)
            hh, C_new, n_new, m_new = _mlstm_head(
                q[rows, cols], k[rows, cols], v[rows, cols], gates[rows, h:h + 1], gates[rows, H_B + h:H_B + h + 1],
                C_ref[i, h], n_ref[i, h:h + 1, :], m_ref[i, :, h:h + 1])
            y_ref[rows, cols] = (jax.nn.sigmoid(ob_ref[rows, cols]) * hh).astype(y_ref.dtype)
            C_ref[i, h] = C_new
            n_ref[i, h:h + 1, :] = n_new
            m_ref[i, :, h:h + 1] = m_new


def mlstm(q, k, v, ob, gates, C0, n0, m0, N, L, *, bb, out_dtype):
    c = MLSTM_CHUNK if L % MLSTM_CHUNK == 0 else L
    nc = L // c
    d = DK_B
    assert bb == 1 or nc == 1
    rows = lambda width: pl.BlockSpec((bb * c, width), lambda n, j: (n * nc + j, 0))
    st_C = pl.BlockSpec((bb, H_B, d, d), lambda n, j: (n, 0, 0, 0))
    st_n = pl.BlockSpec((bb, H_B, d), lambda n, j: (n, 0, 0))
    st_m = pl.BlockSpec((bb, 1, H_B), lambda n, j: (n, 0, 0))
    y, C, n, m = pl.pallas_call(
        functools.partial(_mlstm_kernel, bb=bb, c=c),
        grid=(N // bb, nc),
        in_specs=[rows(H_B * d), rows(H_B * d), rows(H_B * d), rows(H_B * d), rows(gates.shape[1]), st_C, st_n, st_m],
        out_specs=[rows(H_B * d), st_C, st_n, st_m],
        out_shape=[jax.ShapeDtypeStruct((N * L, H_B * d), out_dtype), jax.ShapeDtypeStruct((N, H_B, d, d), F32),
                   jax.ShapeDtypeStruct((N, H_B, d), F32), jax.ShapeDtypeStruct((N, 1, H_B), F32)],
        compiler_params=_cparams("parallel", "arbitrary"),
    )(q, k, v, ob, gates, C0, n0, m0.reshape(N, 1, H_B))
    return y, C, n, m.reshape(N, H_B)


def _diff_lambda(lq1_ref, lk1_ref, lq2_ref, lk2_ref, lam_init):
    return (jnp.exp(jnp.sum(lq1_ref[...] * lk1_ref[...], axis=-1, keepdims=True))
            - jnp.exp(jnp.sum(lq2_ref[...] * lk2_ref[...], axis=-1, keepdims=True)) + lam_init)


def _diff_head_norm(o, gain, lam_init):
    return _rms(o, gain) * (1.0 - lam_init)


def _diff_prompt_kernel(slope_ref, q_ref, k_ref, v_ref, lq1_ref, lk1_ref, lq2_ref, lk2_ref, gain_ref, o_ref,
                        m_s, l_s, acc_s, *, tq, tk, lam_init):
    kvh, qi, ki = pl.program_id(1), pl.program_id(2), pl.program_id(3)
    nk = pl.num_programs(3)
    dv = 2 * DH_C

    @pl.when(ki == 0)
    def _():
        m_s[...] = jnp.full(m_s.shape, -jnp.inf, F32)
        l_s[...] = jnp.zeros(l_s.shape, F32)
        acc_s[...] = jnp.zeros(acc_s.shape, F32)

    @pl.when(ki * tk <= qi * tq + tq - 1)
    def _():
        k = k_ref[...]
        v = v_ref[...]
        dist = ((qi * tq + lax.broadcasted_iota(jnp.int32, (tq, tk), 0))
                - (ki * tk + lax.broadcasted_iota(jnp.int32, (tq, tk), 1)))
        keep = dist >= 0
        distf = dist.astype(F32)
        for g in range(G_C):
            bias = slope_ref[0, kvh * G_C + g] * distf
            for mm in range(2):
                idx = g * 2 + mm
                qh = q_ref[:, g * dv + mm * DH_C:g * dv + (mm + 1) * DH_C]
                sc = _dot_nt(qh, k[:, mm * DH_C:(mm + 1) * DH_C])
                logits = jnp.where(keep, sc - bias, -jnp.inf)
                m_old = m_s[idx]
                m_new = jnp.maximum(m_old, jnp.max(logits, axis=-1, keepdims=True))
                alpha = jnp.exp(m_old - m_new)
                p = jnp.exp(logits - m_new)
                l_s[idx] = alpha * l_s[idx] + jnp.sum(p, axis=-1, keepdims=True)
                acc_s[idx] = alpha * acc_s[idx] + _dot(p.astype(BF16), v)
                m_s[idx] = m_new

    @pl.when(ki == nk - 1)
    def _():
        lam = _diff_lambda(lq1_ref, lk1_ref, lq2_ref, lk2_ref, lam_init)
        for g in range(G_C):
            o = acc_s[2 * g] / l_s[2 * g] - lam * (acc_s[2 * g + 1] / l_s[2 * g + 1])
            o_ref[:, g * dv:(g + 1) * dv] = _diff_head_norm(o, gain_ref[...], lam_init).astype(o_ref.dtype)


def diff_prompt(q, k, v, lams, gain, lam_init, B, S, *, tq=256, tk=512):
    dv = 2 * DH_C
    nq, nk = S // tq, S // tk
    last = lambda qi: (qi * tq + tq - 1) // tk
    vec = lambda a: a.reshape(1, -1)
    return pl.pallas_call(
        functools.partial(_diff_prompt_kernel, tq=tq, tk=tk, lam_init=lam_init),
        grid=(B, KVH_C, nq, nk),
        in_specs=[pl.BlockSpec(memory_space=pltpu.SMEM),
                  pl.BlockSpec((tq, G_C * dv), lambda b, h, qi, ki: (b * nq + qi, h)),
                  pl.BlockSpec((tk, dv), lambda b, h, qi, ki: (b * nk + jnp.minimum(ki, last(qi)), h)),
                  pl.BlockSpec((tk, dv), lambda b, h, qi, ki: (b * nk + jnp.minimum(ki, last(qi)), h)),
                  _resident((1, DH_C)), _resident((1, DH_C)), _resident((1, DH_C)), _resident((1, DH_C)),
                  _resident((1, dv))],
        out_specs=pl.BlockSpec((tq, G_C * dv), lambda b, h, qi, ki: (b * nq + qi, h)),
        out_shape=jax.ShapeDtypeStruct(q.shape, BF16),
        scratch_shapes=[pltpu.VMEM((2 * G_C, tq, 1), F32), pltpu.VMEM((2 * G_C, tq, 1), F32),
                        pltpu.VMEM((2 * G_C, tq, dv), F32)],
        compiler_params=_cparams("parallel", "parallel", "parallel", "arbitrary"),
    )(jnp.asarray([[_alibi_slope(h, H_C) for h in range(H_C)]], F32), q, k, v, *[vec(a) for a in lams], vec(gain))


def _diff_sample_kernel(pt_ref, q_ref, kn_ref, vn_ref, lq1_ref, lk1_ref, lq2_ref, lk2_ref, gain_ref, *rest,
                        pp, T, past, lam_init):
    k_refs, v_refs = rest[:pp], rest[pp:2 * pp]
    o_ref, qbd_s, m_s, l_s, acc_s = rest[2 * pp:]
    j = pl.program_id(1)
    nj = pl.num_programs(1)
    dv = 2 * DH_C
    RK = 2 * G_C * T
    R = KVH_C * RK
    row = lax.broadcasted_iota(jnp.int32, (R, 1), 0)
    t_row = row % T
    head_row = (row // RK) * G_C + (row // T) % G_C
    slope = jnp.exp2(-8.0 * (head_row + 1).astype(F32) / H_C)

    @pl.when(j == 0)
    def _():
        qbd_s[...] = jnp.zeros(qbd_s.shape, F32)
        for kvh in range(KVH_C):
            for mm in range(2):
                for g in range(G_C):
                    r0 = kvh * RK + (mm * G_C + g) * T
                    c0 = (kvh * 2 + mm) * DH_C
                    q0 = ((kvh * G_C + g) * 2 + mm) * DH_C
                    qbd_s[r0:r0 + T, c0:c0 + DH_C] = q_ref[:, q0:q0 + DH_C]
        m_s[...] = jnp.full(m_s.shape, -jnp.inf, F32)
        l_s[...] = jnp.zeros(l_s.shape, F32)
        acc_s[...] = jnp.zeros(acc_s.shape, F32)

    qbd = qbd_s[...].astype(BF16)

    def update(kb, vb, dist, keep):
        logits = _dot_nt(qbd, kb) - slope * dist.astype(F32)
        if keep is not None:
            logits = jnp.where(keep, logits, -jnp.inf)
        m_old = m_s[...]
        m_new = jnp.maximum(m_old, jnp.max(logits, axis=-1, keepdims=True))
        alpha = jnp.exp(m_old - m_new)
        p = jnp.exp(logits - m_new)
        l_s[...] = alpha * l_s[...] + jnp.sum(p, axis=-1, keepdims=True)
        m_s[...] = m_new
        p = p.astype(BF16)
        for kvh in range(KVH_C):
            rows = slice(kvh * RK, (kvh + 1) * RK)
            acc_s[rows, :] = alpha[rows, :] * acc_s[rows, :] + _dot(p[rows, :], vb[:, kvh * dv:(kvh + 1) * dv])

    kb = jnp.concatenate([r[...].astype(BF16) for r in k_refs], axis=0)
    vb = jnp.concatenate([r[...].astype(BF16) for r in v_refs], axis=0)
    kpos = j * (pp * PAGE) + lax.broadcasted_iota(jnp.int32, (R, pp * PAGE), 1)
    update(kb, vb, past + t_row - kpos, None)

    @pl.when(j == nj - 1)
    def _():
        pad = jnp.zeros((PAGE - T, kn_ref.shape[1]), F32)
        kn = jnp.concatenate([kn_ref[...], pad], axis=0).astype(BF16)
        vn = jnp.concatenate([vn_ref[...], pad], axis=0).astype(BF16)
        s_new = lax.broadcasted_iota(jnp.int32, (R, PAGE), 1)
        update(kn, vn, t_row - s_new, s_new <= t_row)
        lam = _diff_lambda(lq1_ref, lk1_ref, lq2_ref, lk2_ref, lam_init)
        on = acc_s[...] / l_s[...]
        half = G_C * T
        for kvh in range(KVH_C):
            o = on[kvh * RK:kvh * RK + half, :] - lam * on[kvh * RK + half:(kvh + 1) * RK, :]
            o = _diff_head_norm(o, gain_ref[...], lam_init)
            for g in range(G_C):
                h = kvh * G_C + g
                o_ref[:, h * dv:(h + 1) * dv] = o[g * T:(g + 1) * T, :]


def diff_sample(q, kn, vn, cache_k, cache_v, page_table, lams, gain, lam_init, N, T, *, pp=8):
    dv = 2 * DH_C
    n_pages = page_table.shape[1]
    assert n_pages % pp == 0
    R = KVH_C * 2 * G_C * T
    width = cache_k.shape[2]
    vec = lambda a: a.reshape(1, -1)
    page = lambda p: pl.BlockSpec((None, PAGE, width), lambda b, j, pt: (pt[b, j * pp + p], 0, 0))
    rows = lambda w: pl.BlockSpec((T, w), lambda b, j, pt: (b, 0))
    const = lambda w: pl.BlockSpec((1, w), lambda b, j, pt: (0, 0))
    grid_spec = pltpu.PrefetchScalarGridSpec(
        num_scalar_prefetch=1,
        grid=(N, n_pages // pp),
        in_specs=([rows(q.shape[1]), rows(width), rows(width), const(DH_C), const(DH_C), const(DH_C), const(DH_C),
                   const(dv)] + [page(p) for p in range(pp)] + [page(p) for p in range(pp)]),
        out_specs=rows(q.shape[1]),
        scratch_shapes=[pltpu.VMEM((R, width), F32), pltpu.VMEM((R, 1), F32), pltpu.VMEM((R, 1), F32),
                        pltpu.VMEM((R, dv), F32)],
    )
    return pl.pallas_call(
        functools.partial(_diff_sample_kernel, pp=pp, T=T, past=n_pages * PAGE, lam_init=lam_init),
        grid_spec=grid_spec,
        out_shape=jax.ShapeDtypeStruct(q.shape, F32),
        compiler_params=_cparams("parallel", "arbitrary"),
    )(page_table, q, kn, vn, *[vec(a) for a in lams], vec(gain), *([cache_k] * pp), *([cache_v] * pp))


def _cross_kernel(q_ref, mk_ref, mv_ref, o_ref, *, bb, tq):
    xdh = q_ref.shape[1] // XH
    for i in range(bb):
        rows = slice(i * tq, (i + 1) * tq)
        q = q_ref[rows, :].astype(BF16)
        mk = mk_ref[i].astype(BF16)
        mv = mv_ref[i].astype(BF16)
        for h in range(XH):
            cols = slice(h * xdh, (h + 1) * xdh)
            s = _dot_nt(q[:, cols], mk[:, cols])
            e = jnp.exp(s - jnp.max(s, axis=-1, keepdims=True))
            p = e * (1.0 / jnp.sum(e, axis=-1, keepdims=True))
            o_ref[rows, cols] = _dot(p.astype(BF16), mv[:, cols]).astype(o_ref.dtype)


def cross_core(q, mk, mv, N, L, *, bb, tq, out_dtype):
    D = q.shape[1]
    M = mk.shape[1]
    nt = L // tq
    assert bb == 1 or nt == 1
    return pl.pallas_call(
        functools.partial(_cross_kernel, bb=bb, tq=tq),
        grid=(N // bb, nt),
        in_specs=[pl.BlockSpec((bb * tq, D), lambda n, i: (n * nt + i, 0)),
                  pl.BlockSpec((bb, M, D), lambda n, i: (n, 0, 0)),
                  pl.BlockSpec((bb, M, D), lambda n, i: (n, 0, 0))],
        out_specs=pl.BlockSpec((bb * tq, D), lambda n, i: (n * nt + i, 0)),
        out_shape=jax.ShapeDtypeStruct(q.shape, out_dtype),
        compiler_params=_cparams("parallel", "parallel"),
    )(q, mk, mv)


def _lambda_init(layer):
    return 0.8 - 0.6 * math.exp(-0.3 * layer)


def _ab_segments(act_dtype):
    qa, kv, hb = H_A * DH_A, 2 * KVH_A * DH_A, H_B * DK_B
    o = 0
    segs = []
    for width, dtype, scale in ((qa, act_dtype, DH_A ** -0.5), (kv, F32, 1.0), (hb, act_dtype, 1.0),
                                (hb, act_dtype, DK_B ** -0.5), (hb, act_dtype, 1.0), (hb, F32, 1.0)):
        segs.append(Seg(o, width, dtype, scale))
        o += width
    segs.append(Seg(o, 128, F32, 1.0, True))
    return segs, o + 128


def kernel(x_prompt, x_sample, mem_prompt, cache_swa_k, cache_swa_v, state_mlstm_C, state_mlstm_n, state_mlstm_m, cache_diff_k, cache_diff_v, page_table, cache_mem_k, cache_mem_v, norm_mix, norm_cross, norm_ffn, norm_final, w_in_ab, b_mlstm_i, b_mlstm_f, attn_sinks, w_out_ab, w_in_c, lambda_q1, lambda_k1, lambda_q2, lambda_k2, diff_norm, w_out_c, w_xq, w_xk, w_xv, w_xo, w_gate, w_up, w_down):
    B, S, D = x_prompt.shape
    NS, TS, _ = x_sample.shape
    M = mem_prompt.shape[1]
    depth = norm_mix.shape[0]
    TMP = 512
    TMS = min(512, NS * TS)
    xdh = D // XH
    bf = lambda a: a.astype(BF16)

    xp = x_prompt.reshape(B * S, D)
    xs = x_sample.reshape(NS * TS, D)
    mem = mem_prompt.reshape(B * M, D)
    zero_bias = lambda n: jnp.zeros((1, n), F32)

    outs = {k: [] for k in ("swa_kp", "swa_vp", "swa_ks", "swa_vs", "C_p", "n_p", "m_p", "C_s", "n_s", "m_s",
                            "dk_p", "dv_p", "dk_s", "dv_s", "mk_p", "mv_p")}
    for layer in range(depth):
        if layer % 2 == 0:
            e = layer // 2
            segs_p, n_cols = _ab_segments(BF16)
            segs_s, _ = _ab_segments(F32)
            w_in = bf(jnp.pad(w_in_ab[e], ((0, 0), (0, n_cols - w_in_ab.shape[2]))))
            bias = jnp.zeros((1, n_cols), F32).at[0, n_cols - 128:n_cols - 128 + 2 * H_B].set(
                jnp.concatenate([b_mlstm_i[e], b_mlstm_f[e]]))
            nqa = H_A * DH_A
            w_out_a, w_out_b = bf(w_out_ab[e, :nqa]), bf(w_out_ab[e, nqa:])
            nk = KVH_A * DH_A

            qa, kva, qb, kb, vb, ob, gates = norm_matmul(xp, norm_mix[layer], w_in, bias, segs_p, tm=TMP)
            ya = swa_prompt(qa, kva, attn_sinks[e], B, S)
            yb, C, n, m = mlstm(qb, kb, vb, ob, gates, jnp.zeros((B, H_B, DK_B, DK_B), F32),
                                jnp.zeros((B, H_B, DK_B), F32), jnp.zeros((B, H_B), F32), B, S, bb=1, out_dtype=BF16)
            xp = matmul_residual([ya, yb], [w_out_a, w_out_b], xp, tm=TMP)
            kv_last = kva.reshape(B, S, 2 * nk)[:, S - WINDOW:]
            outs["swa_kp"].append(kv_last[..., :nk].reshape(B, WINDOW, KVH_A, DH_A))
            outs["swa_vp"].append(kv_last[..., nk:].reshape(B, WINDOW, KVH_A, DH_A))
            outs["C_p"].append(C); outs["n_p"].append(n); outs["m_p"].append(m)

            qa, kva, qb, kb, vb, ob, gates = norm_matmul(xs, norm_mix[layer], w_in, bias, segs_s, tm=TMS)
            ya, k_new, v_new = swa_sample(qa, kva, cache_swa_k[e].reshape(NS, WINDOW, nk),
                                          cache_swa_v[e].reshape(NS, WINDOW, nk), attn_sinks[e], NS, TS)
            yb, C, n, m = mlstm(qb, kb, vb, ob, gates, state_mlstm_C[e], state_mlstm_n[e], state_mlstm_m[e], NS, TS,
                                bb=8, out_dtype=F32)
            xs = matmul_residual([ya, yb], [w_out_a, w_out_b], xs, tm=TMS)
            outs["swa_ks"].append(k_new.reshape(NS, WINDOW, KVH_A, DH_A))
            outs["swa_vs"].append(v_new.reshape(NS, WINDOW, KVH_A, DH_A))
            outs["C_s"].append(C); outs["n_s"].append(n); outs["m_s"].append(m)
        else:
            c = layer // 2
            lam_init = _lambda_init(layer)
            lams = (lambda_q1[c], lambda_k1[c], lambda_q2[c], lambda_k2[c])
            nq, nkv = H_C * 2 * DH_C, KVH_C * 2 * DH_C
            w_in = bf(w_in_c[c])
            w_out = bf(w_out_c[c])
            bias = zero_bias(nq + 2 * nkv)

            segs = [Seg(0, nq, BF16, DH_C ** -0.5), Seg(nq, nkv, F32), Seg(nq + nkv, nkv, F32),
                    Seg(nq, nkv, BF16), Seg(nq + nkv, nkv, BF16)]
            q, k, v, k16, v16 = norm_matmul(xp, norm_mix[layer], w_in, bias, segs, tm=TMP)
            o = diff_prompt(q, k16, v16, lams, diff_norm[c], lam_init, B, S)
            xp = matmul_residual([o], [w_out], xp, tm=TMP)
            outs["dk_p"].append(k.reshape(B, S, KVH_C, 2, DH_C))
            outs["dv_p"].append(v.reshape(B, S, KVH_C, 2 * DH_C))

            segs = [Seg(0, nq, F32, DH_C ** -0.5), Seg(nq, nkv, F32), Seg(nq + nkv, nkv, F32)]
            q, k, v = norm_matmul(xs, norm_mix[layer], w_in, bias, segs, tm=TMS)
            n_pool = cache_diff_k.shape[1]
            o = diff_sample(q, k, v, cache_diff_k[c].reshape(n_pool, PAGE, nkv), cache_diff_v[c].reshape(n_pool, PAGE, nkv),
                            page_table, lams, diff_norm[c], lam_init, NS, TS)
            xs = matmul_residual([o], [w_out], xs, tm=TMS)
            outs["dk_s"].append(k.reshape(NS, TS, KVH_C, 2, DH_C))
            outs["dv_s"].append(v.reshape(NS, TS, KVH_C, 2 * DH_C))

        w_kv = bf(jnp.concatenate([w_xk[layer], w_xv[layer]], axis=1))
        segs = [Seg(0, D, F32), Seg(D, D, F32), Seg(0, D, BF16), Seg(D, D, BF16)]
        mk, mv, mk16, mv16 = norm_matmul(mem, jnp.ones((D,), F32), w_kv, zero_bias(2 * D), segs, tm=TMP, norm=False)
        outs["mk_p"].append(mk.reshape(B, M, XH, xdh))
        outs["mv_p"].append(mv.reshape(B, M, XH, xdh))
        w_q, w_o = bf(w_xq[layer]), bf(w_xo[layer])

        (q,) = norm_matmul(xp, norm_cross[layer], w_q, zero_bias(D), [Seg(0, D, BF16, xdh ** -0.5)], tm=TMP)
        o = cross_core(q, mk16.reshape(B, M, D), mv16.reshape(B, M, D), B, S, bb=1, tq=512, out_dtype=BF16)
        xp = matmul_residual([o], [w_o], xp, tm=TMP)

        (q,) = norm_matmul(xs, norm_cross[layer], w_q, zero_bias(D), [Seg(0, D, F32, xdh ** -0.5)], tm=TMS)
        o = cross_core(q, cache_mem_k[layer].reshape(NS, M, D), cache_mem_v[layer].reshape(NS, M, D), NS, TS,
                       bb=8, tq=TS, out_dtype=F32)
        xs = matmul_residual([o], [w_o], xs, tm=TMS)

        last = layer == depth - 1
        wg, wu, wd = bf(w_gate[layer]), bf(w_up[layer]), bf(w_down[layer])
        xp = ffn(xp, norm_ffn[layer], wg, wu, wd, norm_final, tm=TMP, final_norm=last)
        xs = ffn(xs, norm_ffn[layer], wg, wu, wd, norm_final, tm=TMS, final_norm=last)

    st = jnp.stack
    return (xp.reshape(B, S, D), xs.reshape(NS, TS, D),
            st(outs["swa_kp"]), st(outs["swa_vp"]), st(outs["swa_ks"]), st(outs["swa_vs"]),
            st(outs["C_p"]), st(outs["n_p"]), st(outs["m_p"]), st(outs["C_s"]), st(outs["n_s"]), st(outs["m_s"]),
            st(outs["dk_p"]), st(outs["dv_p"]), st(outs["dk_s"]), st(outs["dv_s"]),
            st(outs["mk_p"]), st(outs["mv_p"]))
```

```python
import functools
import math
from typing import NamedTuple

import jax
import jax.numpy as jnp
from jax import lax
from jax.experimental import pallas as pl
from jax.experimental.pallas import tpu as pltpu

F32 = jnp.float32
BF16 = jnp.bfloat16
EPS = 1e-6

WINDOW = 128
DH_A, KVH_A, G_A = 64, 2, 4
H_A = KVH_A * G_A
H_B, DK_B = 4, 128
MLSTM_CHUNK = 128
DH_C, KVH_C, G_C = 64, 2, 4
H_C = KVH_C * G_C
XH = 4
PAGE = 128

VMEM_LIMIT = 56 * 1024 * 1024


def _alibi_slope(h, n):
    return 2.0 ** (-8.0 * (h + 1) / n)


def _cparams(*sem):
    return pltpu.CompilerParams(dimension_semantics=sem, vmem_limit_bytes=VMEM_LIMIT)


def _resident(shape):
    return pl.BlockSpec(shape, lambda *_: (0,) * len(shape), pipeline_mode=pl.Buffered(1))


def _dot(a, b):
    return jnp.dot(a, b, preferred_element_type=F32)


def _dot_nt(a, b):
    return lax.dot_general(a, b, (((1,), (1,)), ((), ())), preferred_element_type=F32)


def _dot_tn(a, b):
    return lax.dot_general(a, b, (((0,), (0,)), ((), ())), preferred_element_type=F32)


def _rms(x, g):
    return x * lax.rsqrt(jnp.mean(x * x, axis=-1, keepdims=True) + EPS) * g


class Seg(NamedTuple):
    start: int
    width: int
    dtype: object
    scale: float = 1.0
    bias: bool = False
    pos: bool = False


POS_SPLIT = 256


def _split_pos(pos):
    return (pos // POS_SPLIT * POS_SPLIT).astype(F32), (pos % POS_SPLIT).astype(F32)


def _norm_matmul_kernel(x_ref, g_ref, w_ref, b_ref, uh_ref, ul_ref, *out_refs, segs, norm, seq_len):
    x = x_ref[...]
    tm = x.shape[0]
    h = (_rms(x, g_ref[...]) if norm else x).astype(BF16)
    for o_ref, s in zip(out_refs, segs):
        cols = slice(s.start, s.start + s.width)
        z = _dot(h, w_ref[:, cols])
        if s.bias:
            z = z + b_ref[:, cols]
        if s.pos:
            row = pl.program_id(0) * tm + lax.broadcasted_iota(jnp.int32, (tm, 1), 0)
            pos_hi, pos_lo = _split_pos(row % seq_len)
            z = z + pos_hi * uh_ref[:, cols] + pos_lo * ul_ref[:, cols]
        if s.scale != 1.0:
            z = z * s.scale
        o_ref[...] = z.astype(o_ref.dtype)


def norm_matmul(x, gain, w, bias, segs, *, tm, norm=True, pos=None):
    T, D = x.shape
    N = w.shape[1]
    assert T % tm == 0
    u_hi, u_lo, seq_len = pos if pos is not None else (jnp.zeros((1, N), F32), jnp.zeros((1, N), F32), T)
    return pl.pallas_call(
        functools.partial(_norm_matmul_kernel, segs=tuple(segs), norm=norm, seq_len=seq_len),
        grid=(T // tm,),
        in_specs=[pl.BlockSpec((tm, D), lambda i: (i, 0)), _resident((1, D)), _resident((D, N)), _resident((1, N)),
                  _resident((1, N)), _resident((1, N))],
        out_specs=[pl.BlockSpec((tm, s.width), lambda i: (i, 0)) for s in segs],
        out_shape=[jax.ShapeDtypeStruct((T, s.width), s.dtype) for s in segs],
        compiler_params=_cparams("parallel"),
    )(x, gain.reshape(1, D), w, bias, u_hi, u_lo)


def _matmul_residual_kernel(*refs, n_in):
    a_refs, w_refs, x_ref, o_ref = refs[:n_in], refs[n_in:2 * n_in], refs[2 * n_in], refs[2 * n_in + 1]
    acc = x_ref[...]
    for a_ref, w_ref in zip(a_refs, w_refs):
        acc = acc + _dot(a_ref[...].astype(BF16), w_ref[...])
    o_ref[...] = acc


def matmul_residual(acts, ws, x, *, tm):
    T, D = x.shape
    n_in = len(acts)
    return pl.pallas_call(
        functools.partial(_matmul_residual_kernel, n_in=n_in),
        grid=(T // tm,),
        in_specs=([pl.BlockSpec((tm, a.shape[1]), lambda i: (i, 0)) for a in acts]
                  + [_resident(w.shape) for w in ws] + [pl.BlockSpec((tm, D), lambda i: (i, 0))]),
        out_specs=pl.BlockSpec((tm, D), lambda i: (i, 0)),
        out_shape=jax.ShapeDtypeStruct((T, D), F32),
        compiler_params=_cparams("parallel"),
    )(*acts, *ws, x)


def _ffn_kernel(x_ref, g_ref, wg_ref, wu_ref, wd_ref, gf_ref, o_ref, *, ff_chunk, final_norm):
    x = x_ref[...]
    h = _rms(x, g_ref[...]).astype(BF16)
    acc = x
    for c in range(0, wg_ref.shape[1], ff_chunk):
        gate = _dot(h, wg_ref[:, c:c + ff_chunk])
        up = _dot(h, wu_ref[:, c:c + ff_chunk])
        a = gate * jax.nn.sigmoid(gate) * up
        acc = acc + _dot(a.astype(BF16), wd_ref[c:c + ff_chunk, :])
    o_ref[...] = _rms(acc, gf_ref[...]) if final_norm else acc


def ffn(x, gain, wg, wu, wd, gain_final, *, tm, final_norm, ff_chunk=256):
    T, D = x.shape
    F = wg.shape[1]
    assert F % ff_chunk == 0
    return pl.pallas_call(
        functools.partial(_ffn_kernel, ff_chunk=ff_chunk, final_norm=final_norm),
        grid=(T // tm,),
        in_specs=[pl.BlockSpec((tm, D), lambda i: (i, 0)), _resident((1, D)), _resident((D, F)), _resident((D, F)),
                  _resident((F, D)), _resident((1, D))],
        out_specs=pl.BlockSpec((tm, D), lambda i: (i, 0)),
        out_shape=jax.ShapeDtypeStruct((T, D), F32),
        compiler_params=_cparams("parallel"),
    )(x, gain.reshape(1, D), wg, wu, wd, gain_final.reshape(1, D))


def _sink_softmax(pieces, sink):
    m = sink
    for lg in pieces:
        m = jnp.maximum(m, jnp.max(lg, axis=-1, keepdims=True))
    es = [jnp.exp(lg - m) for lg in pieces]
    den = jnp.exp(sink - m)
    for e in es:
        den = den + jnp.sum(e, axis=-1, keepdims=True)
    inv = 1.0 / den
    return [e * inv for e in es]


SWA_GROUP = 8


def _swa_prompt_kernel(sink_ref, q_ref, kvp_ref, kvo_ref, o_ref):
    W = WINDOW
    j = pl.program_id(1)
    q = q_ref[...]
    kv = jnp.concatenate([kvp_ref[...], kvo_ref[...]], axis=0).astype(BF16)
    t = lax.broadcasted_iota(jnp.int32, (W, 2 * W), 0)
    s = lax.broadcasted_iota(jnp.int32, (W, 2 * W), 1)
    dist = t + W - s
    valid = (dist >= 0) & (dist <= W) & ((s >= W) | (j > 0))
    distf = dist.astype(F32)

    def chain(h):
        kh = h // G_A
        sc = _dot_nt(q[:, h * DH_A:(h + 1) * DH_A], kv[:, kh * DH_A:(kh + 1) * DH_A])
        yield
        logits = jnp.where(valid, sc - _alibi_slope(h, H_A) * distf, -jnp.inf)
        (p,) = _sink_softmax([logits], sink_ref[0, h])
        o = _dot(p.astype(BF16), kv[:, (KVH_A + kh) * DH_A:(KVH_A + kh + 1) * DH_A])
        yield
        o_ref[:, h * DH_A:(h + 1) * DH_A] = o.astype(o_ref.dtype)

    for h0 in range(0, H_A, SWA_GROUP):
        _run_staged([chain(h) for h in range(h0, h0 + SWA_GROUP)])


def swa_prompt(q, kv, sinks, B, S):
    W = WINDOW
    nb = S // W
    return pl.pallas_call(
        _swa_prompt_kernel,
        grid=(B, nb),
        in_specs=[pl.BlockSpec(memory_space=pltpu.SMEM),
                  pl.BlockSpec((W, q.shape[1]), lambda b, j: (b * nb + j, 0)),
                  pl.BlockSpec((W, kv.shape[1]), lambda b, j: (b * nb + jnp.maximum(j - 1, 0), 0)),
                  pl.BlockSpec((W, kv.shape[1]), lambda b, j: (b * nb + j, 0))],
        out_specs=pl.BlockSpec((W, q.shape[1]), lambda b, j: (b * nb + j, 0)),
        out_shape=jax.ShapeDtypeStruct(q.shape, BF16),
        compiler_params=_cparams("parallel", "parallel"),
    )(sinks.reshape(1, H_A), q, kv, kv)


def _swa_sample_kernel(sink_ref, q_ref, kvn_ref, kc_ref, vc_ref, o_ref, ko_ref, vo_ref, *, bb, T):
    W = WINDOW
    R = G_A * T
    row = lax.broadcasted_iota(jnp.int32, (R, 1), 0)
    t_row = row % T
    g_row = row // T
    s_c = lax.broadcasted_iota(jnp.int32, (R, W), 1)
    s_n = lax.broadcasted_iota(jnp.int32, (R, T), 1)
    dist_c = (t_row + W - s_c).astype(F32)
    dist_n = (t_row - s_n).astype(F32)
    valid_c = s_c >= t_row
    valid_n = s_n <= t_row
    nk = KVH_A * DH_A
    slopes, sinks = [], []
    for kh in range(KVH_A):
        slope = jnp.zeros((R, 1), F32)
        sink = jnp.zeros((R, 1), F32)
        for g in range(G_A):
            slope = jnp.where(g_row == g, _alibi_slope(kh * G_A + g, H_A), slope)
            sink = jnp.where(g_row == g, sink_ref[0, kh * G_A + g], sink)
        slopes.append(slope)
        sinks.append(sink)

    def chain(i, kh):
        rows = slice(i * T, (i + 1) * T)
        cols = slice(kh * DH_A, (kh + 1) * DH_A)
        q = q_ref[rows, kh * G_A * DH_A:(kh + 1) * G_A * DH_A]
        qs = jnp.concatenate([q[:, g * DH_A:(g + 1) * DH_A] for g in range(G_A)], axis=0).astype(BF16)
        kn = kvn_ref[rows, cols].astype(BF16)
        vn = kvn_ref[rows, nk + kh * DH_A:nk + (kh + 1) * DH_A].astype(BF16)
        lc = _dot_nt(qs, kc_ref[i, :, cols].astype(BF16))
        ln = _dot_nt(qs, kn)
        yield
        lc = jnp.where(valid_c, lc - slopes[kh] * dist_c, -jnp.inf)
        ln = jnp.where(valid_n, ln - slopes[kh] * dist_n, -jnp.inf)
        pc, pn = _sink_softmax([lc, ln], sinks[kh])
        o = _dot(pc.astype(BF16), vc_ref[i, :, cols].astype(BF16)) + _dot(pn.astype(BF16), vn)
        yield
        for g in range(G_A):
            h = kh * G_A + g
            o_ref[rows, h * DH_A:(h + 1) * DH_A] = o[g * T:(g + 1) * T, :]

    for i in range(bb):
        ko_ref[i, 0:W - T, :] = kc_ref[i, T:, :]
        ko_ref[i, W - T:W, :] = kvn_ref[i * T:(i + 1) * T, :nk]
        vo_ref[i, 0:W - T, :] = vc_ref[i, T:, :]
        vo_ref[i, W - T:W, :] = kvn_ref[i * T:(i + 1) * T, nk:]
    group = 4
    for i0 in range(0, bb, group):
        _run_staged([chain(i, kh) for i in range(i0, min(i0 + group, bb)) for kh in range(KVH_A)])


def swa_sample(q, kvn, k_cache, v_cache, sinks, N, T, *, bb=8):
    W = WINDOW
    nk = KVH_A * DH_A
    rows = lambda width: pl.BlockSpec((bb * T, width), lambda i: (i, 0))
    cache = pl.BlockSpec((bb, W, nk), lambda i: (i, 0, 0))
    return pl.pallas_call(
        functools.partial(_swa_sample_kernel, bb=bb, T=T),
        grid=(N // bb,),
        in_specs=[pl.BlockSpec(memory_space=pltpu.SMEM), rows(q.shape[1]), rows(kvn.shape[1]), cache, cache],
        out_specs=[rows(q.shape[1]), cache, cache],
        out_shape=[jax.ShapeDtypeStruct(q.shape, F32), jax.ShapeDtypeStruct(k_cache.shape, F32),
                   jax.ShapeDtypeStruct(v_cache.shape, F32)],
        compiler_params=_cparams("parallel"),
    )(sinks.reshape(1, H_A), q, kvn, k_cache, v_cache)


MLSTM_GROUP_SHORT = 16


def _run_staged(chains):
    live = list(chains)
    while live:
        still = []
        for ch in live:
            try:
                next(ch)
                still.append(ch)
            except StopIteration:
                pass
        live = still


def _mlstm_head(q, k, v, i_col, f_col, C, n, m, emit):
    c = q.shape[0]
    r = lax.broadcasted_iota(jnp.int32, (c, c), 0)
    s = lax.broadcasted_iota(jnp.int32, (c, c), 1)
    eye = r == s
    causal = s <= r
    lf = jax.nn.log_sigmoid(f_col)
    lf_row = jnp.sum(jnp.where(eye, lf, 0.0), axis=0, keepdims=True)
    i_row = jnp.sum(jnp.where(eye, i_col, 0.0), axis=0, keepdims=True)
    b_col = jnp.sum(jnp.where(causal, lf_row, 0.0), axis=1, keepdims=True)
    b_row = jnp.sum(jnp.where(r <= s, lf, 0.0), axis=0, keepdims=True)
    d = jnp.where(causal, b_col - b_row + i_row, -jnp.inf)
    inter = b_col + m
    mt = jnp.maximum(inter, jnp.max(d, axis=1, keepdims=True))
    qb, kb, vb = q.astype(BF16), k.astype(BF16), v.astype(BF16)
    qk = _dot_nt(qb, kb)
    qC = _dot(qb, C.astype(BF16))
    m_new = mt[c - 1:c, :]
    b_last = b_col[c - 1:c, :]
    w = jnp.exp(b_last - b_col + i_col - m_new)
    decay = jnp.exp(b_last + m - m_new)
    kw = k * w
    kv_update = _dot_tn(kw.astype(BF16), vb)
    yield
    sc = qk * jnp.exp(d - mt)
    g = jnp.exp(inter - mt)
    intra = _dot(sc.astype(BF16), vb)
    yield
    num = intra + g * qC
    den = jnp.sum(sc, axis=1, keepdims=True) + g * jnp.sum(q * n, axis=1, keepdims=True)
    h = num / jnp.maximum(jnp.abs(den), jnp.exp(-mt))
    C_new = decay * C + kv_update
    n_new = decay * n + jnp.sum(kw, axis=0, keepdims=True)
    emit(h, C_new, n_new, m_new)


def _mlstm_kernel(q_ref, k_ref, v_ref, ob_ref, g_ref, C0_ref, n0_ref, m0_ref, y_ref, C_ref, n_ref, m_ref, *, bb, c):
    d = DK_B

    @pl.when(pl.program_id(1) == 0)
    def _():
        C_ref[...] = C0_ref[...]
        n_ref[...] = n0_ref[...]
        m_ref[...] = m0_ref[...]

    q = q_ref[...].astype(F32)
    k = k_ref[...].astype(F32)
    v = v_ref[...].astype(F32)
    gates = g_ref[...]

    def chain(i, h):
        rows = slice(i * c, (i + 1) * c)
        cols = slice(h * d, (h + 1) * d)

        def emit(hh, C_new, n_new, m_new):
            y_ref[rows, cols] = (jax.nn.sigmoid(ob_ref[rows, cols]) * hh).astype(y_ref.dtype)
            C_ref[i, h] = C_new
            n_ref[i, h:h + 1, :] = n_new
            m_ref[i, :, h:h + 1] = m_new

        yield from _mlstm_head(q[rows, cols], k[rows, cols], v[rows, cols], gates[rows, h:h + 1],
                               gates[rows, H_B + h:H_B + h + 1], C_ref[i, h], n_ref[i, h:h + 1, :],
                               m_ref[i, :, h:h + 1], emit)

    pairs = [(i, h) for i in range(bb) for h in range(H_B)]
    group = MLSTM_GROUP_SHORT if c <= 64 else 1
    for c0 in range(0, len(pairs), group):
        _run_staged([chain(i, h) for i, h in pairs[c0:c0 + group]])


def mlstm(q, k, v, ob, gates, C0, n0, m0, N, L, *, bb, out_dtype):
    c = MLSTM_CHUNK if L % MLSTM_CHUNK == 0 else L
    nc = L // c
    d = DK_B
    assert bb == 1 or nc == 1
    rows = lambda width: pl.BlockSpec((bb * c, width), lambda n, j: (n * nc + j, 0))
    st_C = pl.BlockSpec((bb, H_B, d, d), lambda n, j: (n, 0, 0, 0))
    st_n = pl.BlockSpec((bb, H_B, d), lambda n, j: (n, 0, 0))
    st_m = pl.BlockSpec((bb, 1, H_B), lambda n, j: (n, 0, 0))
    y, C, n, m = pl.pallas_call(
        functools.partial(_mlstm_kernel, bb=bb, c=c),
        grid=(N // bb, nc),
        in_specs=[rows(H_B * d), rows(H_B * d), rows(H_B * d), rows(H_B * d), rows(gates.shape[1]), st_C, st_n, st_m],
        out_specs=[rows(H_B * d), st_C, st_n, st_m],
        out_shape=[jax.ShapeDtypeStruct((N * L, H_B * d), out_dtype), jax.ShapeDtypeStruct((N, H_B, d, d), F32),
                   jax.ShapeDtypeStruct((N, H_B, d), F32), jax.ShapeDtypeStruct((N, 1, H_B), F32)],
        compiler_params=_cparams("parallel", "arbitrary"),
    )(q, k, v, ob, gates, C0, n0, m0.reshape(N, 1, H_B))
    return y, C, n, m.reshape(N, H_B)


def _diff_lambda(lq1_ref, lk1_ref, lq2_ref, lk2_ref, lam_init):
    return (jnp.exp(jnp.sum(lq1_ref[...] * lk1_ref[...], axis=-1, keepdims=True))
            - jnp.exp(jnp.sum(lq2_ref[...] * lk2_ref[...], axis=-1, keepdims=True)) + lam_init)


def _diff_head_norm(o, gain, lam_init):
    return _rms(o, gain) * (1.0 - lam_init)


def diff_key_extension(n_groups):
    width = 2 * DH_C
    lane = jnp.arange(n_groups * width) % width
    one = lambda sel: jnp.where(sel, 1.0, 0.0).astype(F32).reshape(1, -1)
    return one((lane == DH_C) | (lane == DH_C + 1)), one(lane == DH_C + 2), one(lane == DH_C + 3)


def _diff_prompt_kernel(qi_tab, ki_tab, slope_ref, q_ref, k_ref, v_ref, lq1_ref, lk1_ref, lq2_ref, lk2_ref, gain_ref,
                        o_ref, qx_s, m_s, l_s, acc_s, *, tq, tk, lam_init):
    kvh, step = pl.program_id(1), pl.program_id(2)
    qi, ki = qi_tab[step], ki_tab[step]
    dv = 2 * DH_C

    @pl.when(ki == 0)
    def _():
        m_s[...] = jnp.full(m_s.shape, -jnp.inf, F32)
        l_s[...] = jnp.zeros(l_s.shape, F32)
        acc_s[...] = jnp.zeros(acc_s.shape, F32)
        pos_hi, pos_lo = _split_pos(qi * tq + lax.broadcasted_iota(jnp.int32, (tq, DH_C), 0))
        lane = lax.broadcasted_iota(jnp.int32, (tq, DH_C), 1)
        for g in range(G_C):
            slope = slope_ref[0, kvh * G_C + g]
            ext = jnp.where(lane == 0, -slope * pos_hi,
                            jnp.where(lane == 1, -slope * pos_lo, jnp.where(lane <= 3, slope, 0.0))).astype(BF16)
            for mm in range(2):
                qh = q_ref[:, g * dv + mm * DH_C:g * dv + (mm + 1) * DH_C]
                qx_s[g * 2 + mm] = jnp.concatenate([qh, ext], axis=1)

    def block(masked):
        k = k_ref[...]
        v = v_ref[...]
        if masked:
            keep = ((qi * tq + lax.broadcasted_iota(jnp.int32, (tq, tk), 0))
                    >= (ki * tk + lax.broadcasted_iota(jnp.int32, (tq, tk), 1)))
        n_heads = 2 * G_C

        def scores(idx):
            mm = idx % 2
            return _dot_nt(qx_s[idx], k[:, mm * dv:(mm + 1) * dv])

        nxt = scores(0)
        for idx in range(n_heads):
            logits = nxt
            if idx + 1 < n_heads:
                nxt = scores(idx + 1)
            if masked:
                logits = jnp.where(keep, logits, -jnp.inf)
            m_old = m_s[idx]
            m_new = jnp.maximum(m_old, jnp.max(logits, axis=-1, keepdims=True))
            alpha = jnp.exp(m_old - m_new)
            p = jnp.exp(logits - m_new)
            l_s[idx] = alpha * l_s[idx] + jnp.sum(p, axis=-1, keepdims=True)
            acc_s[idx] = alpha * acc_s[idx] + _dot(p.astype(BF16), v)
            m_s[idx] = m_new

    on_diagonal = ki * tk + tk - 1 > qi * tq
    pl.when(on_diagonal)(lambda: block(True))
    pl.when(jnp.logical_not(on_diagonal))(lambda: block(False))

    @pl.when(ki == (qi * tq + tq - 1) // tk)
    def _():
        lam = _diff_lambda(lq1_ref, lk1_ref, lq2_ref, lk2_ref, lam_init)
        for g in range(G_C):
            o = acc_s[2 * g] / l_s[2 * g] - lam * (acc_s[2 * g + 1] / l_s[2 * g + 1])
            o_ref[:, g * dv:(g + 1) * dv] = _diff_head_norm(o, gain_ref[...], lam_init).astype(o_ref.dtype)


def diff_prompt(q, kx, v, lams, gain, lam_init, B, S, *, tq=512, tk=1024):
    dv = 2 * DH_C
    tk = min(tk, S)
    tq = min(tq, tk)
    assert tk % tq == 0 and S % tk == 0 and math.log2(H_C).is_integer() and H_C <= 8
    nq, nk = S // tq, S // tk
    pairs = [(qi, ki) for qi in range(nq) for ki in range((qi * tq + tq - 1) // tk + 1)]
    qi_tab = jnp.asarray([p[0] for p in pairs], jnp.int32)
    ki_tab = jnp.asarray([p[1] for p in pairs], jnp.int32)
    vec = lambda a: a.reshape(1, -1)
    const = lambda w: pl.BlockSpec((1, w), lambda b, h, s, qt, kt: (0, 0))
    grid_spec = pltpu.PrefetchScalarGridSpec(
        num_scalar_prefetch=2,
        grid=(B, KVH_C, len(pairs)),
        in_specs=[pl.BlockSpec(memory_space=pltpu.SMEM),
                  pl.BlockSpec((tq, G_C * dv), lambda b, h, s, qt, kt: (b * nq + qt[s], h)),
                  pl.BlockSpec((tk, 2 * dv), lambda b, h, s, qt, kt: (b * nk + kt[s], h)),
                  pl.BlockSpec((tk, dv), lambda b, h, s, qt, kt: (b * nk + kt[s], h)),
                  const(DH_C), const(DH_C), const(DH_C), const(DH_C), const(dv)],
        out_specs=pl.BlockSpec((tq, G_C * dv), lambda b, h, s, qt, kt: (b * nq + qt[s], h)),
        scratch_shapes=[pltpu.VMEM((2 * G_C, tq, dv), BF16), pltpu.VMEM((2 * G_C, tq, 1), F32),
                        pltpu.VMEM((2 * G_C, tq, 1), F32), pltpu.VMEM((2 * G_C, tq, dv), F32)],
    )
    return pl.pallas_call(
        functools.partial(_diff_prompt_kernel, tq=tq, tk=tk, lam_init=lam_init),
        grid_spec=grid_spec,
        out_shape=jax.ShapeDtypeStruct(q.shape, BF16),
        compiler_params=_cparams("parallel", "parallel", "arbitrary"),
    )(qi_tab, ki_tab, jnp.asarray([[_alibi_slope(h, H_C) for h in range(H_C)]], F32), q, kx, v,
      *[vec(a) for a in lams], vec(gain))


def _diff_sample_kernel(pt_ref, q_ref, kn_ref, vn_ref, lq1_ref, lk1_ref, lq2_ref, lk2_ref, gain_ref, *rest,
                        pp, T, past, lam_init):
    k_refs, v_refs = rest[:pp], rest[pp:2 * pp]
    o_ref, qbd_s, m_s, l_s, acc_s = rest[2 * pp:]
    j = pl.program_id(1)
    nj = pl.num_programs(1)
    dv = 2 * DH_C
    RK = 2 * G_C * T
    R = KVH_C * RK
    row = lax.broadcasted_iota(jnp.int32, (R, 1), 0)
    t_row = row % T
    head_row = (row // RK) * G_C + (row // T) % G_C
    slope = jnp.exp2(-8.0 * (head_row + 1).astype(F32) / H_C)

    @pl.when(j == 0)
    def _():
        qbd_s[...] = jnp.zeros(qbd_s.shape, F32)
        for kvh in range(KVH_C):
            for mm in range(2):
                for g in range(G_C):
                    r0 = kvh * RK + (mm * G_C + g) * T
                    c0 = (kvh * 2 + mm) * DH_C
                    q0 = ((kvh * G_C + g) * 2 + mm) * DH_C
                    qbd_s[r0:r0 + T, c0:c0 + DH_C] = q_ref[:, q0:q0 + DH_C]
        m_s[...] = jnp.full(m_s.shape, -jnp.inf, F32)
        l_s[...] = jnp.zeros(l_s.shape, F32)
        acc_s[...] = jnp.zeros(acc_s.shape, F32)

    qbd = qbd_s[...].astype(BF16)

    def update(scores, values, dist, keep):
        logits = scores - slope * dist.astype(F32)
        if keep is not None:
            logits = jnp.where(keep, logits, -jnp.inf)
        m_old = m_s[...]
        m_new = jnp.maximum(m_old, jnp.max(logits, axis=-1, keepdims=True))
        alpha = jnp.exp(m_old - m_new)
        p = jnp.exp(logits - m_new)
        l_s[...] = alpha * l_s[...] + jnp.sum(p, axis=-1, keepdims=True)
        m_s[...] = m_new
        p = p.astype(BF16)
        for kvh in range(KVH_C):
            rows = slice(kvh * RK, (kvh + 1) * RK)
            acc_s[rows, :] = alpha[rows, :] * acc_s[rows, :] + _dot(p[rows, :], values(kvh))

    kb = jnp.concatenate([r[...].astype(BF16) for r in k_refs], axis=1)
    page_values = lambda kvh: jnp.concatenate(
        [r[pl.ds(kvh, PAGE, stride=KVH_C), :].astype(BF16) for r in v_refs], axis=0)
    kpos = j * (pp * PAGE) + lax.broadcasted_iota(jnp.int32, (R, pp * PAGE), 1)
    update(_dot(qbd, kb), page_values, past + t_row - kpos, None)

    @pl.when(j == nj - 1)
    def _():
        pad = jnp.zeros((PAGE - T, kn_ref.shape[1]), F32)
        kn = jnp.concatenate([kn_ref[...], pad], axis=0).astype(BF16)
        vn = jnp.concatenate([vn_ref[...], pad], axis=0).astype(BF16)
        s_new = lax.broadcasted_iota(jnp.int32, (R, PAGE), 1)
        update(_dot_nt(qbd, kn), lambda kvh: vn[:, kvh * dv:(kvh + 1) * dv], t_row - s_new, s_new <= t_row)
        lam = _diff_lambda(lq1_ref, lk1_ref, lq2_ref, lk2_ref, lam_init)
        on = acc_s[...] / l_s[...]
        half = G_C * T
        for kvh in range(KVH_C):
            o = on[kvh * RK:kvh * RK + half, :] - lam * on[kvh * RK + half:(kvh + 1) * RK, :]
            o = _diff_head_norm(o, gain_ref[...], lam_init)
            for g in range(G_C):
                h = kvh * G_C + g
                o_ref[:, h * dv:(h + 1) * dv] = o[g * T:(g + 1) * T, :]


def diff_sample(q, kn, vn, cache_kt, cache_v, page_table, lams, gain, lam_init, N, T, *, pp=16):
    dv = 2 * DH_C
    n_pages = page_table.shape[1]
    pp = math.gcd(pp, n_pages)
    R = KVH_C * 2 * G_C * T
    width = KVH_C * 2 * DH_C
    assert cache_kt.shape[1:] == (width, PAGE) and cache_v.shape[1:] == (PAGE * KVH_C, dv)
    vec = lambda a: a.reshape(1, -1)
    page = lambda p, shape: pl.BlockSpec((None,) + shape, lambda b, j, pt: (pt[b, j * pp + p], 0, 0))
    rows = lambda w: pl.BlockSpec((T, w), lambda b, j, pt: (b, 0))
    const = lambda w: pl.BlockSpec((1, w), lambda b, j, pt: (0, 0))
    grid_spec = pltpu.PrefetchScalarGridSpec(
        num_scalar_prefetch=1,
        grid=(N, n_pages // pp),
        in_specs=([rows(q.shape[1]), rows(width), rows(width), const(DH_C), const(DH_C), const(DH_C), const(DH_C),
                   const(dv)] + [page(p, (width, PAGE)) for p in range(pp)]
                  + [page(p, (PAGE * KVH_C, dv)) for p in range(pp)]),
        out_specs=rows(q.shape[1]),
        scratch_shapes=[pltpu.VMEM((R, width), F32), pltpu.VMEM((R, 1), F32), pltpu.VMEM((R, 1), F32),
                        pltpu.VMEM((R, dv), F32)],
    )
    return pl.pallas_call(
        functools.partial(_diff_sample_kernel, pp=pp, T=T, past=n_pages * PAGE, lam_init=lam_init),
        grid_spec=grid_spec,
        out_shape=jax.ShapeDtypeStruct(q.shape, F32),
        compiler_params=_cparams("parallel", "arbitrary"),
    )(page_table, q, kn, vn, *[vec(a) for a in lams], vec(gain), *([cache_kt] * pp), *([cache_v] * pp))


LANES = 128


def _cross_kernel(q_ref, mk_ref, mv_ref, o_ref, *, bb, tq, interleaved):
    D = q_ref.shape[1]
    nch = D // XH // LANES
    if interleaved:
        M = mk_ref.shape[1] // (XH * nch)
        chunk = lambda ref, i, h, c: ref[i, pl.ds(c * XH + h, M, stride=XH * nch), :]
    else:
        chunk = lambda ref, i, h, c: ref[i, :, (h * nch + c) * LANES:(h * nch + c + 1) * LANES]
    col = lambda h, c: slice((h * nch + c) * LANES, (h * nch + c + 1) * LANES)
    rows = lambda i: slice(i * tq, (i + 1) * tq)

    def scores(i, h):
        q = q_ref[rows(i), h * nch * LANES:(h + 1) * nch * LANES].astype(BF16)
        return sum(_dot_nt(q[:, c * LANES:(c + 1) * LANES], chunk(mk_ref, i, h, c).astype(BF16)) for c in range(nch))

    def probs(s):
        e = jnp.exp(s - jnp.max(s, axis=-1, keepdims=True))
        return (e * (1.0 / jnp.sum(e, axis=-1, keepdims=True))).astype(BF16)

    def output(i, h, p):
        for c in range(nch):
            o_ref[rows(i), col(h, c)] = _dot(p, chunk(mv_ref, i, h, c).astype(BF16)).astype(o_ref.dtype)

    pairs = [(i, h) for i in range(bb) for h in range(XH)]
    group = 2 * XH if tq <= 64 else 1
    for g0 in range(0, len(pairs), group):
        grp = pairs[g0:g0 + group]
        ss = [scores(i, h) for i, h in grp]
        ps = [probs(s) for s in ss]
        for (i, h), p in zip(grp, ps):
            output(i, h, p)


def interleave_heads(mem):
    depth, N, M, XH_, xdh = mem.shape
    nch = xdh // LANES
    return jnp.transpose(mem.reshape(depth, N, M, XH_, nch, LANES), (0, 1, 2, 4, 3, 5)).reshape(
        depth, N, M * XH_ * nch, LANES)


def cross_core(q, mk, mv, N, L, *, bb, tq, out_dtype, layer=None):
    D = q.shape[1]
    nt = L // tq
    assert bb == 1 or nt == 1
    if layer is None:
        mem = pl.BlockSpec((bb,) + mk.shape[1:], lambda n, i: (n, 0, 0))
    else:
        mem = pl.BlockSpec((None, bb) + mk.shape[2:], lambda n, i: (layer, n, 0, 0))
    return pl.pallas_call(
        functools.partial(_cross_kernel, bb=bb, tq=tq, interleaved=layer is not None),
        grid=(N // bb, nt),
        in_specs=[pl.BlockSpec((bb * tq, D), lambda n, i: (n * nt + i, 0)), mem, mem],
        out_specs=pl.BlockSpec((bb * tq, D), lambda n, i: (n * nt + i, 0)),
        out_shape=jax.ShapeDtypeStruct(q.shape, out_dtype),
        compiler_params=_cparams("parallel", "parallel"),
    )(q, mk, mv)


def _lambda_init(layer):
    return 0.8 - 0.6 * math.exp(-0.3 * layer)


def _ab_segments(act_dtype):
    qa, kv, hb = H_A * DH_A, 2 * KVH_A * DH_A, H_B * DK_B
    o = 0
    segs = []
    for width, dtype, scale in ((qa, act_dtype, DH_A ** -0.5), (kv, F32, 1.0), (hb, act_dtype, 1.0),
                                (hb, act_dtype, DK_B ** -0.5), (hb, act_dtype, 1.0), (hb, F32, 1.0)):
        segs.append(Seg(o, width, dtype, scale))
        o += width
    segs.append(Seg(o, 128, F32, 1.0, True))
    return segs, o + 128


def kernel(x_prompt, x_sample, mem_prompt, cache_swa_k, cache_swa_v, state_mlstm_C, state_mlstm_n, state_mlstm_m, cache_diff_k, cache_diff_v, page_table, cache_mem_k, cache_mem_v, norm_mix, norm_cross, norm_ffn, norm_final, w_in_ab, b_mlstm_i, b_mlstm_f, attn_sinks, w_out_ab, w_in_c, lambda_q1, lambda_k1, lambda_q2, lambda_k2, diff_norm, w_out_c, w_xq, w_xk, w_xv, w_xo, w_gate, w_up, w_down):
    B, S, D = x_prompt.shape
    NS, TS, _ = x_sample.shape
    M = mem_prompt.shape[1]
    depth = norm_mix.shape[0]
    TMP = 512
    TMS = min(512, NS * TS)
    xdh = D // XH
    bf = lambda a: a.astype(BF16)

    xp = x_prompt.reshape(B * S, D)
    xs = x_sample.reshape(NS * TS, D)
    mem = mem_prompt.reshape(B * M, D)
    zero_bias = lambda n: jnp.zeros((1, n), F32)

    outs = {k: [] for k in ("swa_kp", "swa_vp", "swa_ks", "swa_vs", "C_p", "n_p", "m_p", "C_s", "n_s", "m_s",
                            "dk_p", "dv_p", "dk_s", "dv_s", "mk_p", "mv_p")}
    for layer in range(depth):
        if layer % 2 == 0:
            e = layer // 2
            segs_p, n_cols = _ab_segments(BF16)
            segs_s, _ = _ab_segments(F32)
            w_in = bf(jnp.pad(w_in_ab[e], ((0, 0), (0, n_cols - w_in_ab.shape[2]))))
            bias = jnp.zeros((1, n_cols), F32).at[0, n_cols - 128:n_cols - 128 + 2 * H_B].set(
                jnp.concatenate([b_mlstm_i[e], b_mlstm_f[e]]))
            nqa = H_A * DH_A
            w_out_a, w_out_b = bf(w_out_ab[e, :nqa]), bf(w_out_ab[e, nqa:])
            nk = KVH_A * DH_A

            qa, kva, qb, kb, vb, ob, gates = norm_matmul(xp, norm_mix[layer], w_in, bias, segs_p, tm=TMP)
            ya = swa_prompt(qa, kva, attn_sinks[e], B, S)
            yb, C, n, m = mlstm(qb, kb, vb, ob, gates, jnp.zeros((B, H_B, DK_B, DK_B), F32),
                                jnp.zeros((B, H_B, DK_B), F32), jnp.zeros((B, H_B), F32), B, S, bb=1, out_dtype=BF16)
            xp = matmul_residual([ya, yb], [w_out_a, w_out_b], xp, tm=TMP)
            kv_last = kva.reshape(B, S, 2 * nk)[:, S - WINDOW:]
            outs["swa_kp"].append(kv_last[..., :nk].reshape(B, WINDOW, KVH_A, DH_A))
            outs["swa_vp"].append(kv_last[..., nk:].reshape(B, WINDOW, KVH_A, DH_A))
            outs["C_p"].append(C); outs["n_p"].append(n); outs["m_p"].append(m)

            qa, kva, qb, kb, vb, ob, gates = norm_matmul(xs, norm_mix[layer], w_in, bias, segs_s, tm=TMS)
            ya, k_new, v_new = swa_sample(qa, kva, cache_swa_k[e].reshape(NS, WINDOW, nk),
                                          cache_swa_v[e].reshape(NS, WINDOW, nk), attn_sinks[e], NS, TS)
            yb, C, n, m = mlstm(qb, kb, vb, ob, gates, state_mlstm_C[e], state_mlstm_n[e], state_mlstm_m[e], NS, TS,
                                bb=8, out_dtype=F32)
            xs = matmul_residual([ya, yb], [w_out_a, w_out_b], xs, tm=TMS)
            outs["swa_ks"].append(k_new.reshape(NS, WINDOW, KVH_A, DH_A))
            outs["swa_vs"].append(v_new.reshape(NS, WINDOW, KVH_A, DH_A))
            outs["C_s"].append(C); outs["n_s"].append(n); outs["m_s"].append(m)
        else:
            c = layer // 2
            lam_init = _lambda_init(layer)
            lams = (lambda_q1[c], lambda_k1[c], lambda_q2[c], lambda_k2[c])
            nq, nkv = H_C * 2 * DH_C, KVH_C * 2 * DH_C
            w_in = bf(w_in_c[c])
            w_out = bf(w_out_c[c])

            n_groups = KVH_C * 2
            w_kx = jnp.pad(w_in[:, nq:nq + nkv].reshape(D, n_groups, DH_C), ((0, 0), (0, 0), (0, DH_C)))
            w_px = jnp.concatenate([w_in[:, :nq], w_kx.reshape(D, 2 * nkv), w_in[:, nq:]], axis=1)
            ones, u_hi, u_lo = diff_key_extension(n_groups)
            widen = lambda a: jnp.pad(a, ((0, 0), (nq, 2 * nkv)))
            o_kx, o_k, o_v = nq, nq + 2 * nkv, nq + 3 * nkv
            segs = [Seg(0, nq, BF16, DH_C ** -0.5), Seg(o_kx, 2 * nkv, BF16, 1.0, True, True), Seg(o_k, nkv, F32),
                    Seg(o_v, nkv, F32), Seg(o_v, nkv, BF16)]
            q, kx, k, v, v16 = norm_matmul(xp, norm_mix[layer], w_px, widen(ones), segs, tm=TMP,
                                           pos=(widen(u_hi), widen(u_lo), S))
            o = diff_prompt(q, kx, v16, lams, diff_norm[c], lam_init, B, S)
            xp = matmul_residual([o], [w_out], xp, tm=TMP)
            outs["dk_p"].append(k.reshape(B, S, KVH_C, 2, DH_C))
            outs["dv_p"].append(v.reshape(B, S, KVH_C, 2 * DH_C))

            segs = [Seg(0, nq, F32, DH_C ** -0.5), Seg(nq, nkv, F32), Seg(nq + nkv, nkv, F32)]
            q, k, v = norm_matmul(xs, norm_mix[layer], w_in, zero_bias(nq + 2 * nkv), segs, tm=TMS)
            n_pool = cache_diff_k.shape[1]
            cache_kt = jnp.transpose(cache_diff_k[c], (0, 2, 3, 4, 1)).reshape(n_pool, nkv, PAGE)
            cache_vr = cache_diff_v[c].reshape(n_pool, PAGE * KVH_C, 2 * DH_C)
            o = diff_sample(q, k, v, cache_kt, cache_vr, page_table, lams, diff_norm[c], lam_init, NS, TS)
            xs = matmul_residual([o], [w_out], xs, tm=TMS)
            outs["dk_s"].append(k.reshape(NS, TS, KVH_C, 2, DH_C))
            outs["dv_s"].append(v.reshape(NS, TS, KVH_C, 2 * DH_C))

        w_kv = bf(jnp.concatenate([w_xk[layer], w_xv[layer]], axis=1))
        segs = [Seg(0, D, F32), Seg(D, D, F32), Seg(0, D, BF16), Seg(D, D, BF16)]
        mk, mv, mk16, mv16 = norm_matmul(mem, jnp.ones((D,), F32), w_kv, zero_bias(2 * D), segs, tm=TMP, norm=False)
        outs["mk_p"].append(mk.reshape(B, M, XH, xdh))
        outs["mv_p"].append(mv.reshape(B, M, XH, xdh))
        w_q, w_o = bf(w_xq[layer]), bf(w_xo[layer])

        (q,) = norm_matmul(xp, norm_cross[layer], w_q, zero_bias(D), [Seg(0, D, BF16, xdh ** -0.5)], tm=TMP)
        o = cross_core(q, mk16.reshape(B, M, D), mv16.reshape(B, M, D), B, S, bb=1, tq=512, out_dtype=BF16)
        xp = matmul_residual([o], [w_o], xp, tm=TMP)

        (q,) = norm_matmul(xs, norm_cross[layer], w_q, zero_bias(D), [Seg(0, D, F32, xdh ** -0.5)], tm=TMS)
        o = cross_core(q, interleave_heads(cache_mem_k), interleave_heads(cache_mem_v), NS, TS, bb=8, tq=TS,
                       out_dtype=F32, layer=layer)
        xs = matmul_residual([o], [w_o], xs, tm=TMS)

        last = layer == depth - 1
        wg, wu, wd = bf(w_gate[layer]), bf(w_up[layer]), bf(w_down[layer])
        xp = ffn(xp, norm_ffn[layer], wg, wu, wd, norm_final, tm=TMP, final_norm=last)
        xs = ffn(xs, norm_ffn[layer], wg, wu, wd, norm_final, tm=TMS, final_norm=last)

    st = jnp.stack
    return (xp.reshape(B, S, D), xs.reshape(NS, TS, D),
            st(outs["swa_kp"]), st(outs["swa_vp"]), st(outs["swa_ks"]), st(outs["swa_vs"]),
            st(outs["C_p"]), st(outs["n_p"]), st(outs["m_p"]), st(outs["C_s"]), st(outs["n_s"]), st(outs["m_s"]),
            st(outs["dk_p"]), st(outs["dv_p"]), st(outs["dk_s"]), st(outs["dv_s"]),
            st(outs["mk_p"]), st(outs["mv_p"]))
```

```python
import functools
import math
from typing import NamedTuple

import jax
import jax.numpy as jnp
from jax import lax
from jax.experimental import pallas as pl
from jax.experimental.pallas import tpu as pltpu

F32 = jnp.float32
BF16 = jnp.bfloat16
EPS = 1e-6

WINDOW = 128
DH_A, KVH_A, G_A = 64, 2, 4
H_A = KVH_A * G_A
H_B, DK_B = 4, 128
MLSTM_CHUNK = 128
DH_C, KVH_C, G_C = 64, 2, 4
H_C = KVH_C * G_C
XH = 4
PAGE = 128

VMEM_LIMIT = 56 * 1024 * 1024


def _alibi_slope(h, n):
    return 2.0 ** (-8.0 * (h + 1) / n)


def _cparams(*sem):
    return pltpu.CompilerParams(dimension_semantics=sem, vmem_limit_bytes=VMEM_LIMIT)


def _resident(shape):
    return pl.BlockSpec(shape, lambda *_: (0,) * len(shape), pipeline_mode=pl.Buffered(1))


def _dot(a, b):
    return jnp.dot(a, b, preferred_element_type=F32)


def _dot_nt(a, b):
    return lax.dot_general(a, b, (((1,), (1,)), ((), ())), preferred_element_type=F32)


def _dot_tn(a, b):
    return lax.dot_general(a, b, (((0,), (0,)), ((), ())), preferred_element_type=F32)


def _rms(x, g):
    return x * lax.rsqrt(jnp.mean(x * x, axis=-1, keepdims=True) + EPS) * g


class Seg(NamedTuple):
    start: int
    width: int
    dtype: object
    scale: float = 1.0
    bias: bool = False
    pos: bool = False


POS_SPLIT = 256


def _split_pos(pos):
    return (pos // POS_SPLIT * POS_SPLIT).astype(F32), (pos % POS_SPLIT).astype(F32)


def _norm_matmul_kernel(x_ref, g_ref, w_ref, b_ref, uh_ref, ul_ref, *rest, segs, norm, seq_len, transposed):
    x = x_ref[...]
    tm = x.shape[0]
    h = (_rms(x, g_ref[...]) if norm else x).astype(BF16)
    if transposed:
        wt_ref, rest, ot_ref = rest[0], rest[1:-1], rest[-1]
        ot_ref[...] = _dot_nt(wt_ref[...], h)
    for o_ref, s in zip(rest, segs):
        cols = slice(s.start, s.start + s.width)
        z = _dot(h, w_ref[:, cols])
        if s.bias:
            z = z + b_ref[:, cols]
        if s.pos:
            row = pl.program_id(0) * tm + lax.broadcasted_iota(jnp.int32, (tm, 1), 0)
            pos_hi, pos_lo = _split_pos(row % seq_len)
            z = z + pos_hi * uh_ref[:, cols] + pos_lo * ul_ref[:, cols]
        if s.scale != 1.0:
            z = z * s.scale
        o_ref[...] = z.astype(o_ref.dtype)


def norm_matmul(x, gain, w, bias, segs, *, tm, norm=True, pos=None, w_t=None):
    T, D = x.shape
    N = w.shape[1]
    assert T % tm == 0
    u_hi, u_lo, seq_len = pos if pos is not None else (jnp.zeros((1, N), F32), jnp.zeros((1, N), F32), T)
    in_specs = [pl.BlockSpec((tm, D), lambda i: (i, 0)), _resident((1, D)), _resident((D, N)), _resident((1, N)),
                _resident((1, N)), _resident((1, N))]
    out_specs = [pl.BlockSpec((tm, s.width), lambda i: (i, 0)) for s in segs]
    out_shape = [jax.ShapeDtypeStruct((T, s.width), s.dtype) for s in segs]
    args = [x, gain.reshape(1, D), w, bias, u_hi, u_lo]
    if w_t is not None:
        assert seq_len % tm == 0
        nt = seq_len // tm
        in_specs.append(_resident(w_t.shape))
        out_specs.append(pl.BlockSpec((None, w_t.shape[0], tm), lambda i: (i // nt, 0, i % nt)))
        out_shape.append(jax.ShapeDtypeStruct((T // seq_len, w_t.shape[0], seq_len), F32))
        args.append(w_t)
    return pl.pallas_call(
        functools.partial(_norm_matmul_kernel, segs=tuple(segs), norm=norm, seq_len=seq_len, transposed=w_t is not None),
        grid=(T // tm,),
        in_specs=in_specs, out_specs=out_specs, out_shape=out_shape,
        compiler_params=_cparams("parallel"),
    )(*args)


def _matmul_residual_kernel(*refs, n_in):
    a_refs, w_refs, x_ref, o_ref = refs[:n_in], refs[n_in:2 * n_in], refs[2 * n_in], refs[2 * n_in + 1]
    acc = x_ref[...]
    for a_ref, w_ref in zip(a_refs, w_refs):
        acc = acc + _dot(a_ref[...].astype(BF16), w_ref[...])
    o_ref[...] = acc


def matmul_residual(acts, ws, x, *, tm):
    T, D = x.shape
    n_in = len(acts)
    return pl.pallas_call(
        functools.partial(_matmul_residual_kernel, n_in=n_in),
        grid=(T // tm,),
        in_specs=([pl.BlockSpec((tm, a.shape[1]), lambda i: (i, 0)) for a in acts]
                  + [_resident(w.shape) for w in ws] + [pl.BlockSpec((tm, D), lambda i: (i, 0))]),
        out_specs=pl.BlockSpec((tm, D), lambda i: (i, 0)),
        out_shape=jax.ShapeDtypeStruct((T, D), F32),
        compiler_params=_cparams("parallel"),
    )(*acts, *ws, x)


def _ffn_kernel(x_ref, g_ref, wg_ref, wu_ref, wd_ref, gf_ref, o_ref, *, ff_chunk, final_norm):
    x = x_ref[...]
    h = _rms(x, g_ref[...]).astype(BF16)
    acc = x
    for c in range(0, wg_ref.shape[1], ff_chunk):
        gate = _dot(h, wg_ref[:, c:c + ff_chunk])
        up = _dot(h, wu_ref[:, c:c + ff_chunk])
        a = gate * jax.nn.sigmoid(gate) * up
        acc = acc + _dot(a.astype(BF16), wd_ref[c:c + ff_chunk, :])
    o_ref[...] = _rms(acc, gf_ref[...]) if final_norm else acc


def ffn(x, gain, wg, wu, wd, gain_final, *, tm, final_norm, ff_chunk=256):
    T, D = x.shape
    F = wg.shape[1]
    assert F % ff_chunk == 0
    return pl.pallas_call(
        functools.partial(_ffn_kernel, ff_chunk=ff_chunk, final_norm=final_norm),
        grid=(T // tm,),
        in_specs=[pl.BlockSpec((tm, D), lambda i: (i, 0)), _resident((1, D)), _resident((D, F)), _resident((D, F)),
                  _resident((F, D)), _resident((1, D))],
        out_specs=pl.BlockSpec((tm, D), lambda i: (i, 0)),
        out_shape=jax.ShapeDtypeStruct((T, D), F32),
        compiler_params=_cparams("parallel"),
    )(x, gain.reshape(1, D), wg, wu, wd, gain_final.reshape(1, D))


def _post_mixer_kernel(*refs, n_in, ff_chunk, final_norm):
    a_refs, w_refs = refs[:n_in], refs[n_in:2 * n_in]
    (x_ref, gc_ref, wq_ref, mk_ref, mv_ref, wo_ref, gf_ref, wg_ref, wu_ref, wd_ref, gl_ref, o_ref, att_s) = refs[2 * n_in:]
    x = x_ref[...]
    for a_ref, w_ref in zip(a_refs, w_refs):
        x = x + _dot(a_ref[...].astype(BF16), w_ref[...])
    xdh = x.shape[1] // XH
    h = _rms(x, gc_ref[...]).astype(BF16)
    q = (_dot(h, wq_ref[...]) * xdh ** -0.5).astype(BF16)
    cols = lambda hh: slice(hh * xdh, (hh + 1) * xdh)
    scores = lambda hh: _dot_nt(q[:, cols(hh)], mk_ref[0, :, cols(hh)])
    nxt = scores(0)
    for hh in range(XH):
        s = nxt
        if hh + 1 < XH:
            nxt = scores(hh + 1)
        e = jnp.exp(s - jnp.max(s, axis=-1, keepdims=True))
        p = (e * (1.0 / jnp.sum(e, axis=-1, keepdims=True))).astype(BF16)
        att_s[:, cols(hh)] = _dot(p, mv_ref[0, :, cols(hh)]).astype(BF16)
    x = x + _dot(att_s[...], wo_ref[...])
    h = _rms(x, gf_ref[...]).astype(BF16)
    acc = x
    for c in range(0, wg_ref.shape[1], ff_chunk):
        gate = _dot(h, wg_ref[:, c:c + ff_chunk])
        up = _dot(h, wu_ref[:, c:c + ff_chunk])
        acc = acc + _dot((gate * jax.nn.sigmoid(gate) * up).astype(BF16), wd_ref[c:c + ff_chunk, :])
    o_ref[...] = _rms(acc, gl_ref[...]) if final_norm else acc


def post_mixer(acts, w_outs, x, gain_cross, w_q, mk, mv, w_o, gain_ffn, wg, wu, wd, gain_final, N, L, *, tq, final_norm,
               ff_chunk=256):
    T, D = x.shape
    F = wg.shape[1]
    nt = L // tq
    n_in = len(acts)
    assert L % tq == 0 and F % ff_chunk == 0
    rows = lambda w: pl.BlockSpec((tq, w), lambda n, i: (n * nt + i, 0))
    mem = pl.BlockSpec((1,) + mk.shape[1:], lambda n, i: (n, 0, 0))
    vec = lambda g: g.reshape(1, D)
    return pl.pallas_call(
        functools.partial(_post_mixer_kernel, n_in=n_in, ff_chunk=ff_chunk, final_norm=final_norm),
        grid=(N, nt),
        in_specs=([rows(a.shape[1]) for a in acts] + [_resident(w.shape) for w in w_outs]
                  + [rows(D), _resident((1, D)), _resident((D, D)), mem, mem, _resident((D, D)), _resident((1, D)),
                     _resident((D, F)), _resident((D, F)), _resident((F, D)), _resident((1, D))]),
        out_specs=rows(D),
        out_shape=jax.ShapeDtypeStruct((T, D), F32),
        scratch_shapes=[pltpu.VMEM((tq, D), BF16)],
        compiler_params=_cparams("parallel", "parallel"),
    )(*acts, *w_outs, x, vec(gain_cross), w_q, mk, mv, w_o, vec(gain_ffn), wg, wu, wd, vec(gain_final))


def _sink_softmax(pieces, sink):
    m = sink
    for lg in pieces:
        m = jnp.maximum(m, jnp.max(lg, axis=-1, keepdims=True))
    es = [jnp.exp(lg - m) for lg in pieces]
    den = jnp.exp(sink - m)
    for e in es:
        den = den + jnp.sum(e, axis=-1, keepdims=True)
    inv = 1.0 / den
    return [e * inv for e in es]


SWA_GROUP = 8


def _swa_prompt_kernel(sink_ref, q_ref, kvp_ref, kvo_ref, o_ref):
    W = WINDOW
    j = pl.program_id(1)
    q = q_ref[...]
    kv = jnp.concatenate([kvp_ref[...], kvo_ref[...]], axis=0).astype(BF16)
    t = lax.broadcasted_iota(jnp.int32, (W, 2 * W), 0)
    s = lax.broadcasted_iota(jnp.int32, (W, 2 * W), 1)
    dist = t + W - s
    valid = (dist >= 0) & (dist <= W) & ((s >= W) | (j > 0))
    distf = dist.astype(F32)

    def chain(h):
        kh = h // G_A
        sc = _dot_nt(q[:, h * DH_A:(h + 1) * DH_A], kv[:, kh * DH_A:(kh + 1) * DH_A])
        yield
        logits = jnp.where(valid, sc - _alibi_slope(h, H_A) * distf, -jnp.inf)
        (p,) = _sink_softmax([logits], sink_ref[0, h])
        o = _dot(p.astype(BF16), kv[:, (KVH_A + kh) * DH_A:(KVH_A + kh + 1) * DH_A])
        yield
        o_ref[:, h * DH_A:(h + 1) * DH_A] = o.astype(o_ref.dtype)

    for h0 in range(0, H_A, SWA_GROUP):
        _run_staged([chain(h) for h in range(h0, h0 + SWA_GROUP)])


def swa_prompt(q, kv, sinks, B, S):
    W = WINDOW
    nb = S // W
    return pl.pallas_call(
        _swa_prompt_kernel,
        grid=(B, nb),
        in_specs=[pl.BlockSpec(memory_space=pltpu.SMEM),
                  pl.BlockSpec((W, q.shape[1]), lambda b, j: (b * nb + j, 0)),
                  pl.BlockSpec((W, kv.shape[1]), lambda b, j: (b * nb + jnp.maximum(j - 1, 0), 0)),
                  pl.BlockSpec((W, kv.shape[1]), lambda b, j: (b * nb + j, 0))],
        out_specs=pl.BlockSpec((W, q.shape[1]), lambda b, j: (b * nb + j, 0)),
        out_shape=jax.ShapeDtypeStruct(q.shape, BF16),
        compiler_params=_cparams("parallel", "parallel"),
    )(sinks.reshape(1, H_A), q, kv, kv)


def _swa_sample_kernel(sink_ref, q_ref, kvn_ref, kc_ref, vc_ref, o_ref, ko_ref, vo_ref, *, bb, T):
    W = WINDOW
    R = G_A * T
    row = lax.broadcasted_iota(jnp.int32, (R, 1), 0)
    t_row = row % T
    g_row = row // T
    s_c = lax.broadcasted_iota(jnp.int32, (R, W), 1)
    s_n = lax.broadcasted_iota(jnp.int32, (R, T), 1)
    dist_c = (t_row + W - s_c).astype(F32)
    dist_n = (t_row - s_n).astype(F32)
    valid_c = s_c >= t_row
    valid_n = s_n <= t_row
    nk = KVH_A * DH_A
    slopes, sinks = [], []
    for kh in range(KVH_A):
        slope = jnp.zeros((R, 1), F32)
        sink = jnp.zeros((R, 1), F32)
        for g in range(G_A):
            slope = jnp.where(g_row == g, _alibi_slope(kh * G_A + g, H_A), slope)
            sink = jnp.where(g_row == g, sink_ref[0, kh * G_A + g], sink)
        slopes.append(slope)
        sinks.append(sink)

    def chain(i, kh):
        rows = slice(i * T, (i + 1) * T)
        cols = slice(kh * DH_A, (kh + 1) * DH_A)
        q = q_ref[rows, kh * G_A * DH_A:(kh + 1) * G_A * DH_A]
        qs = jnp.concatenate([q[:, g * DH_A:(g + 1) * DH_A] for g in range(G_A)], axis=0).astype(BF16)
        kn = kvn_ref[rows, cols].astype(BF16)
        vn = kvn_ref[rows, nk + kh * DH_A:nk + (kh + 1) * DH_A].astype(BF16)
        lc = _dot_nt(qs, kc_ref[i, :, cols].astype(BF16))
        ln = _dot_nt(qs, kn)
        yield
        lc = jnp.where(valid_c, lc - slopes[kh] * dist_c, -jnp.inf)
        ln = jnp.where(valid_n, ln - slopes[kh] * dist_n, -jnp.inf)
        pc, pn = _sink_softmax([lc, ln], sinks[kh])
        o = _dot(pc.astype(BF16), vc_ref[i, :, cols].astype(BF16)) + _dot(pn.astype(BF16), vn)
        yield
        for g in range(G_A):
            h = kh * G_A + g
            o_ref[rows, h * DH_A:(h + 1) * DH_A] = o[g * T:(g + 1) * T, :]

    for i in range(bb):
        ko_ref[i, 0:W - T, :] = kc_ref[i, T:, :]
        ko_ref[i, W - T:W, :] = kvn_ref[i * T:(i + 1) * T, :nk]
        vo_ref[i, 0:W - T, :] = vc_ref[i, T:, :]
        vo_ref[i, W - T:W, :] = kvn_ref[i * T:(i + 1) * T, nk:]
    group = 4
    for i0 in range(0, bb, group):
        _run_staged([chain(i, kh) for i in range(i0, min(i0 + group, bb)) for kh in range(KVH_A)])


def swa_sample(q, kvn, k_cache, v_cache, sinks, N, T, *, bb=8):
    W = WINDOW
    nk = KVH_A * DH_A
    rows = lambda width: pl.BlockSpec((bb * T, width), lambda i: (i, 0))
    cache = pl.BlockSpec((bb, W, nk), lambda i: (i, 0, 0))
    return pl.pallas_call(
        functools.partial(_swa_sample_kernel, bb=bb, T=T),
        grid=(N // bb,),
        in_specs=[pl.BlockSpec(memory_space=pltpu.SMEM), rows(q.shape[1]), rows(kvn.shape[1]), cache, cache],
        out_specs=[rows(q.shape[1]), cache, cache],
        out_shape=[jax.ShapeDtypeStruct(q.shape, F32), jax.ShapeDtypeStruct(k_cache.shape, F32),
                   jax.ShapeDtypeStruct(v_cache.shape, F32)],
        compiler_params=_cparams("parallel"),
    )(sinks.reshape(1, H_A), q, kvn, k_cache, v_cache)


MLSTM_GROUP_SHORT = 16


def _run_staged(chains):
    live = list(chains)
    while live:
        still = []
        for ch in live:
            try:
                next(ch)
                still.append(ch)
            except StopIteration:
                pass
        live = still


def _mlstm_head(q, k, v, i_col, f_col, C, n, m, emit):
    c = q.shape[0]
    r = lax.broadcasted_iota(jnp.int32, (c, c), 0)
    s = lax.broadcasted_iota(jnp.int32, (c, c), 1)
    eye = r == s
    causal = s <= r
    lf = jax.nn.log_sigmoid(f_col)
    lf_row = jnp.sum(jnp.where(eye, lf, 0.0), axis=0, keepdims=True)
    i_row = jnp.sum(jnp.where(eye, i_col, 0.0), axis=0, keepdims=True)
    b_col = jnp.sum(jnp.where(causal, lf_row, 0.0), axis=1, keepdims=True)
    b_row = jnp.sum(jnp.where(r <= s, lf, 0.0), axis=0, keepdims=True)
    d = jnp.where(causal, b_col - b_row + i_row, -jnp.inf)
    inter = b_col + m
    mt = jnp.maximum(inter, jnp.max(d, axis=1, keepdims=True))
    qb, kb, vb = q.astype(BF16), k.astype(BF16), v.astype(BF16)
    qk = _dot_nt(qb, kb)
    qC = _dot(qb, C.astype(BF16))
    m_new = mt[c - 1:c, :]
    b_last = b_col[c - 1:c, :]
    w = jnp.exp(b_last - b_col + i_col - m_new)
    decay = jnp.exp(b_last + m - m_new)
    kw = k * w
    kv_update = _dot_tn(kw.astype(BF16), vb)
    yield
    sc = qk * jnp.exp(d - mt)
    g = jnp.exp(inter - mt)
    intra = _dot(sc.astype(BF16), vb)
    yield
    num = intra + g * qC
    den = jnp.sum(sc, axis=1, keepdims=True) + g * jnp.sum(q * n, axis=1, keepdims=True)
    h = num / jnp.maximum(jnp.abs(den), jnp.exp(-mt))
    C_new = decay * C + kv_update
    n_new = decay * n + jnp.sum(kw, axis=0, keepdims=True)
    emit(h, C_new, n_new, m_new)


def _mlstm_kernel(q_ref, k_ref, v_ref, ob_ref, g_ref, C0_ref, n0_ref, m0_ref, y_ref, C_ref, n_ref, m_ref, *, bb, c):
    d = DK_B

    @pl.when(pl.program_id(1) == 0)
    def _():
        C_ref[...] = C0_ref[...]
        n_ref[...] = n0_ref[...]
        m_ref[...] = m0_ref[...]

    q = q_ref[...].astype(F32)
    k = k_ref[...].astype(F32)
    v = v_ref[...].astype(F32)
    gates = g_ref[...]

    def chain(i, h):
        rows = slice(i * c, (i + 1) * c)
        cols = slice(h * d, (h + 1) * d)

        def emit(hh, C_new, n_new, m_new):
            y_ref[rows, cols] = (jax.nn.sigmoid(ob_ref[rows, cols]) * hh).astype(y_ref.dtype)
            C_ref[i, h] = C_new
            n_ref[i, h:h + 1, :] = n_new
            m_ref[i, :, h:h + 1] = m_new

        yield from _mlstm_head(q[rows, cols], k[rows, cols], v[rows, cols], gates[rows, h:h + 1],
                               gates[rows, H_B + h:H_B + h + 1], C_ref[i, h], n_ref[i, h:h + 1, :],
                               m_ref[i, :, h:h + 1], emit)

    pairs = [(i, h) for i in range(bb) for h in range(H_B)]
    group = MLSTM_GROUP_SHORT if c <= 64 else 1
    for c0 in range(0, len(pairs), group):
        _run_staged([chain(i, h) for i, h in pairs[c0:c0 + group]])


def mlstm(q, k, v, ob, gates, C0, n0, m0, N, L, *, bb, out_dtype):
    c = MLSTM_CHUNK if L % MLSTM_CHUNK == 0 else L
    nc = L // c
    d = DK_B
    assert bb == 1 or nc == 1
    rows = lambda width: pl.BlockSpec((bb * c, width), lambda n, j: (n * nc + j, 0))
    st_C = pl.BlockSpec((bb, H_B, d, d), lambda n, j: (n, 0, 0, 0))
    st_n = pl.BlockSpec((bb, H_B, d), lambda n, j: (n, 0, 0))
    st_m = pl.BlockSpec((bb, 1, H_B), lambda n, j: (n, 0, 0))
    y, C, n, m = pl.pallas_call(
        functools.partial(_mlstm_kernel, bb=bb, c=c),
        grid=(N // bb, nc),
        in_specs=[rows(H_B * d), rows(H_B * d), rows(H_B * d), rows(H_B * d), rows(gates.shape[1]), st_C, st_n, st_m],
        out_specs=[rows(H_B * d), st_C, st_n, st_m],
        out_shape=[jax.ShapeDtypeStruct((N * L, H_B * d), out_dtype), jax.ShapeDtypeStruct((N, H_B, d, d), F32),
                   jax.ShapeDtypeStruct((N, H_B, d), F32), jax.ShapeDtypeStruct((N, 1, H_B), F32)],
        compiler_params=_cparams("parallel", "arbitrary"),
    )(q, k, v, ob, gates, C0, n0, m0.reshape(N, 1, H_B))
    return y, C, n, m.reshape(N, H_B)


def _diff_lambda(lq1_ref, lk1_ref, lq2_ref, lk2_ref, lam_init):
    return (jnp.exp(jnp.sum(lq1_ref[...] * lk1_ref[...], axis=-1, keepdims=True))
            - jnp.exp(jnp.sum(lq2_ref[...] * lk2_ref[...], axis=-1, keepdims=True)) + lam_init)


def _diff_head_norm(o, gain, lam_init):
    return _rms(o, gain) * (1.0 - lam_init)


def diff_key_extension(n_groups):
    width = 2 * DH_C
    lane = jnp.arange(n_groups * width) % width
    one = lambda sel: jnp.where(sel, 1.0, 0.0).astype(F32).reshape(1, -1)
    return one((lane == DH_C) | (lane == DH_C + 1)), one(lane == DH_C + 2), one(lane == DH_C + 3)


def _diff_prompt_kernel(qi_tab, ki_tab, slope_ref, q_ref, k_ref, v_ref, lq1_ref, lk1_ref, lq2_ref, lk2_ref, gain_ref,
                        o_ref, qx_s, m_s, l_s, acc_s, *, tq, tk, lam_init):
    kvh, step = pl.program_id(1), pl.program_id(2)
    qi, ki = qi_tab[step], ki_tab[step]
    dv = 2 * DH_C

    @pl.when(ki == 0)
    def _():
        m_s[...] = jnp.full(m_s.shape, -jnp.inf, F32)
        l_s[...] = jnp.zeros(l_s.shape, F32)
        acc_s[...] = jnp.zeros(acc_s.shape, F32)
        pos_hi, pos_lo = _split_pos(qi * tq + lax.broadcasted_iota(jnp.int32, (tq, DH_C), 0))
        lane = lax.broadcasted_iota(jnp.int32, (tq, DH_C), 1)
        for g in range(G_C):
            slope = slope_ref[0, kvh * G_C + g]
            ext = jnp.where(lane == 0, -slope * pos_hi,
                            jnp.where(lane == 1, -slope * pos_lo, jnp.where(lane <= 3, slope, 0.0))).astype(BF16)
            for mm in range(2):
                qh = q_ref[:, g * dv + mm * DH_C:g * dv + (mm + 1) * DH_C]
                qx_s[g * 2 + mm] = jnp.concatenate([qh, ext], axis=1)

    def block(masked):
        k = k_ref[...]
        v = v_ref[...]
        if masked:
            keep = ((qi * tq + lax.broadcasted_iota(jnp.int32, (tq, tk), 0))
                    >= (ki * tk + lax.broadcasted_iota(jnp.int32, (tq, tk), 1)))
        n_heads = 2 * G_C

        def scores(idx):
            mm = idx % 2
            return _dot_nt(qx_s[idx], k[:, mm * dv:(mm + 1) * dv])

        nxt = scores(0)
        for idx in range(n_heads):
            logits = nxt
            if idx + 1 < n_heads:
                nxt = scores(idx + 1)
            if masked:
                logits = jnp.where(keep, logits, -jnp.inf)
            m_old = m_s[idx]
            m_new = jnp.maximum(m_old, jnp.max(logits, axis=-1, keepdims=True))
            alpha = jnp.exp(m_old - m_new)
            p = jnp.exp(logits - m_new)
            l_s[idx] = alpha * l_s[idx] + jnp.sum(p, axis=-1, keepdims=True)
            acc_s[idx] = alpha * acc_s[idx] + _dot(p.astype(BF16), v)
            m_s[idx] = m_new

    on_diagonal = ki * tk + tk - 1 > qi * tq
    pl.when(on_diagonal)(lambda: block(True))
    pl.when(jnp.logical_not(on_diagonal))(lambda: block(False))

    @pl.when(ki == (qi * tq + tq - 1) // tk)
    def _():
        lam = _diff_lambda(lq1_ref, lk1_ref, lq2_ref, lk2_ref, lam_init)
        for g in range(G_C):
            o = acc_s[2 * g] / l_s[2 * g] - lam * (acc_s[2 * g + 1] / l_s[2 * g + 1])
            o_ref[:, g * dv:(g + 1) * dv] = _diff_head_norm(o, gain_ref[...], lam_init).astype(o_ref.dtype)


def diff_prompt(q, kx, v, lams, gain, lam_init, B, S, *, tq=512, tk=1024):
    dv = 2 * DH_C
    tk = min(tk, S)
    tq = min(tq, tk)
    assert tk % tq == 0 and S % tk == 0 and math.log2(H_C).is_integer() and H_C <= 8
    nq, nk = S // tq, S // tk
    pairs = [(qi, ki) for qi in range(nq) for ki in range((qi * tq + tq - 1) // tk + 1)]
    qi_tab = jnp.asarray([p[0] for p in pairs], jnp.int32)
    ki_tab = jnp.asarray([p[1] for p in pairs], jnp.int32)
    vec = lambda a: a.reshape(1, -1)
    const = lambda w: pl.BlockSpec((1, w), lambda b, h, s, qt, kt: (0, 0))
    grid_spec = pltpu.PrefetchScalarGridSpec(
        num_scalar_prefetch=2,
        grid=(B, KVH_C, len(pairs)),
        in_specs=[pl.BlockSpec(memory_space=pltpu.SMEM),
                  pl.BlockSpec((tq, G_C * dv), lambda b, h, s, qt, kt: (b * nq + qt[s], h)),
                  pl.BlockSpec((tk, 2 * dv), lambda b, h, s, qt, kt: (b * nk + kt[s], h)),
                  pl.BlockSpec((tk, dv), lambda b, h, s, qt, kt: (b * nk + kt[s], h)),
                  const(DH_C), const(DH_C), const(DH_C), const(DH_C), const(dv)],
        out_specs=pl.BlockSpec((tq, G_C * dv), lambda b, h, s, qt, kt: (b * nq + qt[s], h)),
        scratch_shapes=[pltpu.VMEM((2 * G_C, tq, dv), BF16), pltpu.VMEM((2 * G_C, tq, 1), F32),
                        pltpu.VMEM((2 * G_C, tq, 1), F32), pltpu.VMEM((2 * G_C, tq, dv), F32)],
    )
    return pl.pallas_call(
        functools.partial(_diff_prompt_kernel, tq=tq, tk=tk, lam_init=lam_init),
        grid_spec=grid_spec,
        out_shape=jax.ShapeDtypeStruct(q.shape, BF16),
        compiler_params=_cparams("parallel", "parallel", "arbitrary"),
    )(qi_tab, ki_tab, jnp.asarray([[_alibi_slope(h, H_C) for h in range(H_C)]], F32), q, kx, v,
      *[vec(a) for a in lams], vec(gain))


def _diff_sample_kernel(pt_ref, q_ref, kn_ref, vn_ref, lq1_ref, lk1_ref, lq2_ref, lk2_ref, gain_ref, kt_hbm, v_hbm,
                        o_ref, kbuf, vbuf, sems, qbd_s, m_s, l_s, acc_s, *, pp, T, past, lam_init):
    b, j = pl.program_id(0), pl.program_id(1)
    nb, nj = pl.num_programs(0), pl.num_programs(1)
    dv = 2 * DH_C

    step = b * nj + j
    slot = step % 2

    def page_copies(bi, ji, sl, read_table=True):
        copies = []
        for p in range(pp):
            page = pt_ref[bi, ji * pp + p] if read_table else 0
            copies.append(pltpu.make_async_copy(kt_hbm.at[page], kbuf.at[sl, p], sems.at[0, sl]))
            copies.append(pltpu.make_async_copy(v_hbm.at[page], vbuf.at[sl, p], sems.at[1, sl]))
        return copies

    @pl.when(step == 0)
    def _():
        for cp in page_copies(b, j, slot):
            cp.start()

    @pl.when(step + 1 < nb * nj)
    def _():
        wrap = j + 1 == nj
        for cp in page_copies(jnp.where(wrap, b + 1, b), jnp.where(wrap, 0, j + 1), 1 - slot):
            cp.start()

    for cp in page_copies(b, j, slot, read_table=False):
        cp.wait()
    k_refs = [kbuf.at[slot, p] for p in range(pp)]
    v_refs = [vbuf.at[slot, p] for p in range(pp)]
    RK = 2 * G_C * T
    R = KVH_C * RK
    row = lax.broadcasted_iota(jnp.int32, (R, 1), 0)
    t_row = row % T
    head_row = (row // RK) * G_C + (row // T) % G_C
    slope = jnp.exp2(-8.0 * (head_row + 1).astype(F32) / H_C)

    @pl.when(j == 0)
    def _():
        qbd_s[...] = jnp.zeros(qbd_s.shape, F32)
        for kvh in range(KVH_C):
            for mm in range(2):
                for g in range(G_C):
                    r0 = kvh * RK + (mm * G_C + g) * T
                    c0 = (kvh * 2 + mm) * DH_C
                    q0 = ((kvh * G_C + g) * 2 + mm) * DH_C
                    qbd_s[r0:r0 + T, c0:c0 + DH_C] = q_ref[:, q0:q0 + DH_C]
        m_s[...] = jnp.full(m_s.shape, -jnp.inf, F32)
        l_s[...] = jnp.zeros(l_s.shape, F32)
        acc_s[...] = jnp.zeros(acc_s.shape, F32)

    qbd = qbd_s[...].astype(BF16)

    def update(scores, values, dist, keep):
        logits = scores - slope * dist.astype(F32)
        if keep is not None:
            logits = jnp.where(keep, logits, -jnp.inf)
        m_old = m_s[...]
        m_new = jnp.maximum(m_old, jnp.max(logits, axis=-1, keepdims=True))
        alpha = jnp.exp(m_old - m_new)
        p = jnp.exp(logits - m_new)
        l_s[...] = alpha * l_s[...] + jnp.sum(p, axis=-1, keepdims=True)
        m_s[...] = m_new
        p = p.astype(BF16)
        for kvh in range(KVH_C):
            rows = slice(kvh * RK, (kvh + 1) * RK)
            acc_s[rows, :] = alpha[rows, :] * acc_s[rows, :] + _dot(p[rows, :], values(kvh))

    kb = jnp.concatenate([r[...].astype(BF16) for r in k_refs], axis=1)
    page_values = lambda kvh: jnp.concatenate(
        [r[pl.ds(kvh, PAGE, stride=KVH_C), :].astype(BF16) for r in v_refs], axis=0)
    kpos = j * (pp * PAGE) + lax.broadcasted_iota(jnp.int32, (R, pp * PAGE), 1)
    update(_dot(qbd, kb), page_values, past + t_row - kpos, None)

    @pl.when(j == nj - 1)
    def _():
        pad = jnp.zeros((PAGE - T, kn_ref.shape[1]), F32)
        kn = jnp.concatenate([kn_ref[...], pad], axis=0).astype(BF16)
        vn = jnp.concatenate([vn_ref[...], pad], axis=0).astype(BF16)
        s_new = lax.broadcasted_iota(jnp.int32, (R, PAGE), 1)
        update(_dot_nt(qbd, kn), lambda kvh: vn[:, kvh * dv:(kvh + 1) * dv], t_row - s_new, s_new <= t_row)
        lam = _diff_lambda(lq1_ref, lk1_ref, lq2_ref, lk2_ref, lam_init)
        on = acc_s[...] / l_s[...]
        half = G_C * T
        for kvh in range(KVH_C):
            o = on[kvh * RK:kvh * RK + half, :] - lam * on[kvh * RK + half:(kvh + 1) * RK, :]
            o = _diff_head_norm(o, gain_ref[...], lam_init)
            for g in range(G_C):
                h = kvh * G_C + g
                o_ref[:, h * dv:(h + 1) * dv] = o[g * T:(g + 1) * T, :]


def diff_sample(q, kn, vn, cache_kt, cache_v, page_table, lams, gain, lam_init, N, T, *, pp=16):
    dv = 2 * DH_C
    n_pages = page_table.shape[1]
    pp = math.gcd(pp, n_pages)
    R = KVH_C * 2 * G_C * T
    width = KVH_C * 2 * DH_C
    assert cache_kt.shape[1:] == (width, PAGE) and cache_v.shape[1:] == (PAGE * KVH_C, dv)
    vec = lambda a: a.reshape(1, -1)
    rows = lambda w: pl.BlockSpec((T, w), lambda b, j, pt: (b, 0))
    const = lambda w: pl.BlockSpec((1, w), lambda b, j, pt: (0, 0))
    in_hbm = pl.BlockSpec(memory_space=pl.ANY)
    grid_spec = pltpu.PrefetchScalarGridSpec(
        num_scalar_prefetch=1,
        grid=(N, n_pages // pp),
        in_specs=[rows(q.shape[1]), rows(width), rows(width), const(DH_C), const(DH_C), const(DH_C), const(DH_C),
                  const(dv), in_hbm, in_hbm],
        out_specs=rows(q.shape[1]),
        scratch_shapes=[pltpu.VMEM((2, pp, width, PAGE), F32), pltpu.VMEM((2, pp, PAGE * KVH_C, dv), F32),
                        pltpu.SemaphoreType.DMA((2, 2)),
                        pltpu.VMEM((R, width), F32), pltpu.VMEM((R, 1), F32), pltpu.VMEM((R, 1), F32),
                        pltpu.VMEM((R, dv), F32)],
    )
    return pl.pallas_call(
        functools.partial(_diff_sample_kernel, pp=pp, T=T, past=n_pages * PAGE, lam_init=lam_init),
        grid_spec=grid_spec,
        out_shape=jax.ShapeDtypeStruct(q.shape, F32),
        compiler_params=_cparams("arbitrary", "arbitrary"),
    )(page_table, q, kn, vn, *[vec(a) for a in lams], vec(gain), cache_kt, cache_v)


LANES = 128


def _cross_kernel(q_ref, mk_ref, mv_ref, o_ref, *, bb, tq, interleaved):
    D = q_ref.shape[1]
    nch = D // XH // LANES
    if interleaved:
        M = mk_ref.shape[1] // (XH * nch)
        chunk = lambda ref, i, h, c: ref[i, pl.ds(c * XH + h, M, stride=XH * nch), :]
    else:
        chunk = lambda ref, i, h, c: ref[i, :, (h * nch + c) * LANES:(h * nch + c + 1) * LANES]
    col = lambda h, c: slice((h * nch + c) * LANES, (h * nch + c + 1) * LANES)
    rows = lambda i: slice(i * tq, (i + 1) * tq)

    def scores(i, h):
        q = q_ref[rows(i), h * nch * LANES:(h + 1) * nch * LANES].astype(BF16)
        return sum(_dot_nt(q[:, c * LANES:(c + 1) * LANES], chunk(mk_ref, i, h, c).astype(BF16)) for c in range(nch))

    def probs(s):
        e = jnp.exp(s - jnp.max(s, axis=-1, keepdims=True))
        return (e * (1.0 / jnp.sum(e, axis=-1, keepdims=True))).astype(BF16)

    def output(i, h, p):
        for c in range(nch):
            o_ref[rows(i), col(h, c)] = _dot(p, chunk(mv_ref, i, h, c).astype(BF16)).astype(o_ref.dtype)

    pairs = [(i, h) for i in range(bb) for h in range(XH)]
    group = 2 * XH if tq <= 64 else 1
    for g0 in range(0, len(pairs), group):
        grp = pairs[g0:g0 + group]
        ss = [scores(i, h) for i, h in grp]
        ps = [probs(s) for s in ss]
        for (i, h), p in zip(grp, ps):
            output(i, h, p)


def interleave_heads(mem):
    depth, N, M, XH_, xdh = mem.shape
    nch = xdh // LANES
    return jnp.transpose(mem.reshape(depth, N, M, XH_, nch, LANES), (0, 1, 2, 4, 3, 5)).reshape(
        depth, N, M * XH_ * nch, LANES)


def cross_core(q, mk, mv, N, L, *, bb, tq, out_dtype, layer=None):
    D = q.shape[1]
    nt = L // tq
    assert bb == 1 or nt == 1
    if layer is None:
        mem = pl.BlockSpec((bb,) + mk.shape[1:], lambda n, i: (n, 0, 0))
    else:
        mem = pl.BlockSpec((None, bb) + mk.shape[2:], lambda n, i: (layer, n, 0, 0))
    return pl.pallas_call(
        functools.partial(_cross_kernel, bb=bb, tq=tq, interleaved=layer is not None),
        grid=(N // bb, nt),
        in_specs=[pl.BlockSpec((bb * tq, D), lambda n, i: (n * nt + i, 0)), mem, mem],
        out_specs=pl.BlockSpec((bb * tq, D), lambda n, i: (n * nt + i, 0)),
        out_shape=jax.ShapeDtypeStruct(q.shape, out_dtype),
        compiler_params=_cparams("parallel", "parallel"),
    )(q, mk, mv)


def _lambda_init(layer):
    return 0.8 - 0.6 * math.exp(-0.3 * layer)


def _ab_segments(act_dtype):
    qa, kv, hb = H_A * DH_A, 2 * KVH_A * DH_A, H_B * DK_B
    o = 0
    segs = []
    for width, dtype, scale in ((qa, act_dtype, DH_A ** -0.5), (kv, F32, 1.0), (hb, act_dtype, 1.0),
                                (hb, act_dtype, DK_B ** -0.5), (hb, act_dtype, 1.0), (hb, F32, 1.0)):
        segs.append(Seg(o, width, dtype, scale))
        o += width
    segs.append(Seg(o, 128, F32, 1.0, True))
    return segs, o + 128


def kernel(x_prompt, x_sample, mem_prompt, cache_swa_k, cache_swa_v, state_mlstm_C, state_mlstm_n, state_mlstm_m, cache_diff_k, cache_diff_v, page_table, cache_mem_k, cache_mem_v, norm_mix, norm_cross, norm_ffn, norm_final, w_in_ab, b_mlstm_i, b_mlstm_f, attn_sinks, w_out_ab, w_in_c, lambda_q1, lambda_k1, lambda_q2, lambda_k2, diff_norm, w_out_c, w_xq, w_xk, w_xv, w_xo, w_gate, w_up, w_down):
    B, S, D = x_prompt.shape
    NS, TS, _ = x_sample.shape
    M = mem_prompt.shape[1]
    depth = norm_mix.shape[0]
    TMP = 512
    TMS = min(512, NS * TS)
    xdh = D // XH
    bf = lambda a: a.astype(BF16)

    xp = x_prompt.reshape(B * S, D)
    xs = x_sample.reshape(NS * TS, D)
    mem = mem_prompt.reshape(B * M, D)
    zero_bias = lambda n: jnp.zeros((1, n), F32)

    outs = {k: [] for k in ("swa_kp", "swa_vp", "swa_ks", "swa_vs", "C_p", "n_p", "m_p", "C_s", "n_s", "m_s",
                            "dk_p", "dv_p", "dk_s", "dv_s", "mk_p", "mv_p")}
    for layer in range(depth):
        if layer % 2 == 0:
            e = layer // 2
            segs_p, n_cols = _ab_segments(BF16)
            segs_s, _ = _ab_segments(F32)
            w_in = bf(jnp.pad(w_in_ab[e], ((0, 0), (0, n_cols - w_in_ab.shape[2]))))
            bias = jnp.zeros((1, n_cols), F32).at[0, n_cols - 128:n_cols - 128 + 2 * H_B].set(
                jnp.concatenate([b_mlstm_i[e], b_mlstm_f[e]]))
            nqa = H_A * DH_A
            w_out_a, w_out_b = bf(w_out_ab[e, :nqa]), bf(w_out_ab[e, nqa:])
            nk = KVH_A * DH_A

            qa, kva, qb, kb, vb, ob, gates = norm_matmul(xp, norm_mix[layer], w_in, bias, segs_p, tm=TMP)
            ya = swa_prompt(qa, kva, attn_sinks[e], B, S)
            yb, C, n, m = mlstm(qb, kb, vb, ob, gates, jnp.zeros((B, H_B, DK_B, DK_B), F32),
                                jnp.zeros((B, H_B, DK_B), F32), jnp.zeros((B, H_B), F32), B, S, bb=1, out_dtype=BF16)
            mix_p = ([ya, yb], [w_out_a, w_out_b])
            kv_last = kva.reshape(B, S, 2 * nk)[:, S - WINDOW:]
            outs["swa_kp"].append(kv_last[..., :nk].reshape(B, WINDOW, KVH_A, DH_A))
            outs["swa_vp"].append(kv_last[..., nk:].reshape(B, WINDOW, KVH_A, DH_A))
            outs["C_p"].append(C); outs["n_p"].append(n); outs["m_p"].append(m)

            qa, kva, qb, kb, vb, ob, gates = norm_matmul(xs, norm_mix[layer], w_in, bias, segs_s, tm=TMS)
            ya, k_new, v_new = swa_sample(qa, kva, cache_swa_k[e].reshape(NS, WINDOW, nk),
                                          cache_swa_v[e].reshape(NS, WINDOW, nk), attn_sinks[e], NS, TS)
            yb, C, n, m = mlstm(qb, kb, vb, ob, gates, state_mlstm_C[e], state_mlstm_n[e], state_mlstm_m[e], NS, TS,
                                bb=8, out_dtype=F32)
            xs = matmul_residual([ya, yb], [w_out_a, w_out_b], xs, tm=TMS)
            outs["swa_ks"].append(k_new.reshape(NS, WINDOW, KVH_A, DH_A))
            outs["swa_vs"].append(v_new.reshape(NS, WINDOW, KVH_A, DH_A))
            outs["C_s"].append(C); outs["n_s"].append(n); outs["m_s"].append(m)
        else:
            c = layer // 2
            lam_init = _lambda_init(layer)
            lams = (lambda_q1[c], lambda_k1[c], lambda_q2[c], lambda_k2[c])
            nq, nkv = H_C * 2 * DH_C, KVH_C * 2 * DH_C
            w_in = bf(w_in_c[c])
            w_out = bf(w_out_c[c])

            n_groups = KVH_C * 2
            w_kx = jnp.pad(w_in[:, nq:nq + nkv].reshape(D, n_groups, DH_C), ((0, 0), (0, 0), (0, DH_C)))
            w_px = jnp.concatenate([w_in[:, :nq], w_kx.reshape(D, 2 * nkv), w_in[:, nq + nkv:]], axis=1)
            ones, u_hi, u_lo = diff_key_extension(n_groups)
            widen = lambda a: jnp.pad(a, ((0, 0), (nq, nkv)))
            o_kx, o_v = nq, nq + 2 * nkv
            segs = [Seg(0, nq, BF16, DH_C ** -0.5), Seg(o_kx, 2 * nkv, BF16, 1.0, True, True), Seg(o_v, nkv, F32),
                    Seg(o_v, nkv, BF16)]
            q, kx, v, v16, kt = norm_matmul(xp, norm_mix[layer], w_px, widen(ones), segs, tm=TMP,
                                            pos=(widen(u_hi), widen(u_lo), S), w_t=w_in[:, nq:nq + nkv].T)
            o = diff_prompt(q, kx, v16, lams, diff_norm[c], lam_init, B, S)
            mix_p = ([o], [w_out])
            outs["dk_p"].append(jnp.transpose(kt.reshape(B, KVH_C, 2, DH_C, S), (0, 4, 1, 2, 3)))
            outs["dv_p"].append(v.reshape(B, S, KVH_C, 2 * DH_C))

            segs = [Seg(0, nq, F32, DH_C ** -0.5), Seg(nq, nkv, F32), Seg(nq + nkv, nkv, F32)]
            q, k, v = norm_matmul(xs, norm_mix[layer], w_in, zero_bias(nq + 2 * nkv), segs, tm=TMS)
            n_pool = cache_diff_k.shape[1]
            cache_kt = jnp.transpose(cache_diff_k[c], (0, 2, 3, 4, 1)).reshape(n_pool, nkv, PAGE)
            cache_vr = cache_diff_v[c].reshape(n_pool, PAGE * KVH_C, 2 * DH_C)
            o = diff_sample(q, k, v, cache_kt, cache_vr, page_table, lams, diff_norm[c], lam_init, NS, TS)
            xs = matmul_residual([o], [w_out], xs, tm=TMS)
            outs["dk_s"].append(k.reshape(NS, TS, KVH_C, 2, DH_C))
            outs["dv_s"].append(v.reshape(NS, TS, KVH_C, 2 * DH_C))

        w_kv = bf(jnp.concatenate([w_xk[layer], w_xv[layer]], axis=1))
        segs = [Seg(0, D, F32), Seg(D, D, F32), Seg(0, D, BF16), Seg(D, D, BF16)]
        mk, mv, mk16, mv16 = norm_matmul(mem, jnp.ones((D,), F32), w_kv, zero_bias(2 * D), segs, tm=TMP, norm=False)
        outs["mk_p"].append(mk.reshape(B, M, XH, xdh))
        outs["mv_p"].append(mv.reshape(B, M, XH, xdh))
        w_q, w_o = bf(w_xq[layer]), bf(w_xo[layer])
        last = layer == depth - 1
        wg, wu, wd = bf(w_gate[layer]), bf(w_up[layer]), bf(w_down[layer])

        xp = post_mixer(*mix_p, xp, norm_cross[layer], w_q, mk16.reshape(B, M, D), mv16.reshape(B, M, D), w_o,
                        norm_ffn[layer], wg, wu, wd, norm_final, B, S, tq=TMP, final_norm=last)

        (q,) = norm_matmul(xs, norm_cross[layer], w_q, zero_bias(D), [Seg(0, D, F32, xdh ** -0.5)], tm=TMS)
        o = cross_core(q, interleave_heads(cache_mem_k), interleave_heads(cache_mem_v), NS, TS, bb=8, tq=TS,
                       out_dtype=F32, layer=layer)
        xs = matmul_residual([o], [w_o], xs, tm=TMS)
        xs = ffn(xs, norm_ffn[layer], wg, wu, wd, norm_final, tm=TMS, final_norm=last)

    st = jnp.stack
    return (xp.reshape(B, S, D), xs.reshape(NS, TS, D),
            st(outs["swa_kp"]), st(outs["swa_vp"]), st(outs["swa_ks"]), st(outs["swa_vs"]),
            st(outs["C_p"]), st(outs["n_p"]), st(outs["m_p"]), st(outs["C_s"]), st(outs["n_s"]), st(outs["m_s"]),
            st(outs["dk_p"]), st(outs["dv_p"]), st(outs["dk_s"]), st(outs["dv_s"]),
            st(outs["mk_p"]), st(outs["mv_p"]))
```

```python
import functools
import math
from typing import NamedTuple

import jax
import jax.numpy as jnp
from jax import lax
from jax.experimental import pallas as pl
from jax.experimental.pallas import tpu as pltpu

F32 = jnp.float32
BF16 = jnp.bfloat16
EPS = 1e-6

WINDOW = 128
DH_A, KVH_A, G_A = 64, 2, 4
H_A = KVH_A * G_A
H_B, DK_B = 4, 128
MLSTM_CHUNK = 128
DH_C, KVH_C, G_C = 64, 2, 4
H_C = KVH_C * G_C
XH = 4
PAGE = 128

VMEM_LIMIT = 56 * 1024 * 1024


def _alibi_slope(h, n):
    return 2.0 ** (-8.0 * (h + 1) / n)


def _cparams(*sem):
    return pltpu.CompilerParams(dimension_semantics=sem, vmem_limit_bytes=VMEM_LIMIT)


def _resident(shape):
    return pl.BlockSpec(shape, lambda *_: (0,) * len(shape), pipeline_mode=pl.Buffered(1))


def _dot(a, b):
    return jnp.dot(a, b, preferred_element_type=F32)


def _dot_nt(a, b):
    return lax.dot_general(a, b, (((1,), (1,)), ((), ())), preferred_element_type=F32)


def _dot_tn(a, b):
    return lax.dot_general(a, b, (((0,), (0,)), ((), ())), preferred_element_type=F32)


def _rms(x, g):
    return x * lax.rsqrt(jnp.mean(x * x, axis=-1, keepdims=True) + EPS) * g


class Seg(NamedTuple):
    start: int
    width: int
    dtype: object
    scale: float = 1.0
    bias: bool = False
    pos: bool = False


POS_SPLIT = 256


def _split_pos(pos):
    return (pos // POS_SPLIT * POS_SPLIT).astype(F32), (pos % POS_SPLIT).astype(F32)


def _norm_matmul_kernel(x_ref, g_ref, w_ref, b_ref, uh_ref, ul_ref, *rest, segs, norm, seq_len, n_t):
    x = x_ref[...]
    tm = x.shape[0]
    h = (_rms(x, g_ref[...]) if norm else x).astype(BF16)
    wt_refs, out_refs, ot_refs = rest[:n_t], rest[n_t:len(rest) - n_t], rest[len(rest) - n_t:]
    for wt_ref, ot_ref in zip(wt_refs, ot_refs):
        ot_ref[...] = _dot_nt(wt_ref[...], h).astype(ot_ref.dtype)
    for o_ref, s in zip(out_refs, segs):
        cols = slice(s.start, s.start + s.width)
        z = _dot(h, w_ref[:, cols])
        if s.bias:
            z = z + b_ref[:, cols]
        if s.pos:
            row = pl.program_id(0) * tm + lax.broadcasted_iota(jnp.int32, (tm, 1), 0)
            pos_hi, pos_lo = _split_pos(row % seq_len)
            z = z + pos_hi * uh_ref[:, cols] + pos_lo * ul_ref[:, cols]
        if s.scale != 1.0:
            z = z * s.scale
        o_ref[...] = z.astype(o_ref.dtype)


def norm_matmul(x, gain, w, bias, segs, *, tm, norm=True, pos=None, w_t=()):
    T, D = x.shape
    N = w.shape[1]
    assert T % tm == 0
    u_hi, u_lo, seq_len = pos if pos is not None else (jnp.zeros((1, N), F32), jnp.zeros((1, N), F32), T)
    in_specs = [pl.BlockSpec((tm, D), lambda i: (i, 0)), _resident((1, D)), _resident((D, N)), _resident((1, N)),
                _resident((1, N)), _resident((1, N))]
    out_specs = [pl.BlockSpec((tm, s.width), lambda i: (i, 0)) for s in segs]
    out_shape = [jax.ShapeDtypeStruct((T, s.width), s.dtype) for s in segs]
    args = [x, gain.reshape(1, D), w, bias, u_hi, u_lo]
    assert not w_t or seq_len % tm == 0
    nt = seq_len // tm
    for wt, dtype in w_t:
        in_specs.append(_resident(wt.shape))
        out_specs.append(pl.BlockSpec((None, wt.shape[0], tm), lambda i: (i // nt, 0, i % nt)))
        out_shape.append(jax.ShapeDtypeStruct((T // seq_len, wt.shape[0], seq_len), dtype))
        args.append(wt)
    return pl.pallas_call(
        functools.partial(_norm_matmul_kernel, segs=tuple(segs), norm=norm, seq_len=seq_len, n_t=len(w_t)),
        grid=(T // tm,),
        in_specs=in_specs, out_specs=out_specs, out_shape=out_shape,
        compiler_params=_cparams("parallel"),
    )(*args)


def _matmul_residual_kernel(*refs, n_in):
    a_refs, w_refs, x_ref, o_ref = refs[:n_in], refs[n_in:2 * n_in], refs[2 * n_in], refs[2 * n_in + 1]
    acc = x_ref[...]
    for a_ref, w_ref in zip(a_refs, w_refs):
        acc = acc + _dot(a_ref[...].astype(BF16), w_ref[...])
    o_ref[...] = acc


def matmul_residual(acts, ws, x, *, tm):
    T, D = x.shape
    n_in = len(acts)
    return pl.pallas_call(
        functools.partial(_matmul_residual_kernel, n_in=n_in),
        grid=(T // tm,),
        in_specs=([pl.BlockSpec((tm, a.shape[1]), lambda i: (i, 0)) for a in acts]
                  + [_resident(w.shape) for w in ws] + [pl.BlockSpec((tm, D), lambda i: (i, 0))]),
        out_specs=pl.BlockSpec((tm, D), lambda i: (i, 0)),
        out_shape=jax.ShapeDtypeStruct((T, D), F32),
        compiler_params=_cparams("parallel"),
    )(*acts, *ws, x)


def _ffn_kernel(x_ref, g_ref, wg_ref, wu_ref, wd_ref, gf_ref, o_ref, *, ff_chunk, final_norm):
    x = x_ref[...]
    h = _rms(x, g_ref[...]).astype(BF16)
    acc = x
    for c in range(0, wg_ref.shape[1], ff_chunk):
        gate = _dot(h, wg_ref[:, c:c + ff_chunk])
        up = _dot(h, wu_ref[:, c:c + ff_chunk])
        a = gate * jax.nn.sigmoid(gate) * up
        acc = acc + _dot(a.astype(BF16), wd_ref[c:c + ff_chunk, :])
    o_ref[...] = _rms(acc, gf_ref[...]) if final_norm else acc


def ffn(x, gain, wg, wu, wd, gain_final, *, tm, final_norm, ff_chunk=256):
    T, D = x.shape
    F = wg.shape[1]
    assert F % ff_chunk == 0
    return pl.pallas_call(
        functools.partial(_ffn_kernel, ff_chunk=ff_chunk, final_norm=final_norm),
        grid=(T // tm,),
        in_specs=[pl.BlockSpec((tm, D), lambda i: (i, 0)), _resident((1, D)), _resident((D, F)), _resident((D, F)),
                  _resident((F, D)), _resident((1, D))],
        out_specs=pl.BlockSpec((tm, D), lambda i: (i, 0)),
        out_shape=jax.ShapeDtypeStruct((T, D), F32),
        compiler_params=_cparams("parallel"),
    )(x, gain.reshape(1, D), wg, wu, wd, gain_final.reshape(1, D))


def _post_mixer_kernel(*refs, n_in, ff_chunk, final_norm):
    a_refs, w_refs = refs[:n_in], refs[n_in:2 * n_in]
    (x_ref, gc_ref, wq_ref, mk_ref, mv_ref, wo_ref, gf_ref, wg_ref, wu_ref, wd_ref, gl_ref, o_ref, att_s) = refs[2 * n_in:]
    x = x_ref[...]
    for a_ref, w_ref in zip(a_refs, w_refs):
        x = x + _dot(a_ref[...].astype(BF16), w_ref[...])
    xdh = x.shape[1] // XH
    h = _rms(x, gc_ref[...]).astype(BF16)
    q = (_dot(h, wq_ref[...]) * xdh ** -0.5).astype(BF16)
    cols = lambda hh: slice(hh * xdh, (hh + 1) * xdh)
    scores = lambda hh: _dot_nt(q[:, cols(hh)], mk_ref[0, :, cols(hh)])
    nxt = scores(0)
    for hh in range(XH):
        s = nxt
        if hh + 1 < XH:
            nxt = scores(hh + 1)
        e = jnp.exp(s - jnp.max(s, axis=-1, keepdims=True))
        p = (e * (1.0 / jnp.sum(e, axis=-1, keepdims=True))).astype(BF16)
        att_s[:, cols(hh)] = _dot(p, mv_ref[0, :, cols(hh)]).astype(BF16)
    x = x + _dot(att_s[...], wo_ref[...])
    h = _rms(x, gf_ref[...]).astype(BF16)
    acc = x
    for c in range(0, wg_ref.shape[1], ff_chunk):
        gate = _dot(h, wg_ref[:, c:c + ff_chunk])
        up = _dot(h, wu_ref[:, c:c + ff_chunk])
        acc = acc + _dot((gate * jax.nn.sigmoid(gate) * up).astype(BF16), wd_ref[c:c + ff_chunk, :])
    o_ref[...] = _rms(acc, gl_ref[...]) if final_norm else acc


def post_mixer(acts, w_outs, x, gain_cross, w_q, mk, mv, w_o, gain_ffn, wg, wu, wd, gain_final, N, L, *, tq, final_norm,
               ff_chunk=256):
    T, D = x.shape
    F = wg.shape[1]
    nt = L // tq
    n_in = len(acts)
    assert L % tq == 0 and F % ff_chunk == 0
    rows = lambda w: pl.BlockSpec((tq, w), lambda n, i: (n * nt + i, 0))
    mem = pl.BlockSpec((1,) + mk.shape[1:], lambda n, i: (n, 0, 0))
    vec = lambda g: g.reshape(1, D)
    return pl.pallas_call(
        functools.partial(_post_mixer_kernel, n_in=n_in, ff_chunk=ff_chunk, final_norm=final_norm),
        grid=(N, nt),
        in_specs=([rows(a.shape[1]) for a in acts] + [_resident(w.shape) for w in w_outs]
                  + [rows(D), _resident((1, D)), _resident((D, D)), mem, mem, _resident((D, D)), _resident((1, D)),
                     _resident((D, F)), _resident((D, F)), _resident((F, D)), _resident((1, D))]),
        out_specs=rows(D),
        out_shape=jax.ShapeDtypeStruct((T, D), F32),
        scratch_shapes=[pltpu.VMEM((tq, D), BF16)],
        compiler_params=_cparams("parallel", "parallel"),
    )(*acts, *w_outs, x, vec(gain_cross), w_q, mk, mv, w_o, vec(gain_ffn), wg, wu, wd, vec(gain_final))


def _sink_softmax(pieces, sink):
    m = sink
    for lg in pieces:
        m = jnp.maximum(m, jnp.max(lg, axis=-1, keepdims=True))
    es = [jnp.exp(lg - m) for lg in pieces]
    den = jnp.exp(sink - m)
    for e in es:
        den = den + jnp.sum(e, axis=-1, keepdims=True)
    inv = 1.0 / den
    return [e * inv for e in es]


SWA_GROUP = 8


def _swa_prompt_kernel(sink_ref, q_ref, kvp_ref, kvo_ref, o_ref):
    W = WINDOW
    j = pl.program_id(1)
    q = q_ref[...]
    kv = jnp.concatenate([kvp_ref[...], kvo_ref[...]], axis=0).astype(BF16)
    t = lax.broadcasted_iota(jnp.int32, (W, 2 * W), 0)
    s = lax.broadcasted_iota(jnp.int32, (W, 2 * W), 1)
    dist = t + W - s
    valid = (dist >= 0) & (dist <= W) & ((s >= W) | (j > 0))
    distf = dist.astype(F32)

    def chain(h):
        kh = h // G_A
        sc = _dot_nt(q[:, h * DH_A:(h + 1) * DH_A], kv[:, kh * DH_A:(kh + 1) * DH_A])
        yield
        logits = jnp.where(valid, sc - _alibi_slope(h, H_A) * distf, -jnp.inf)
        (p,) = _sink_softmax([logits], sink_ref[0, h])
        o = _dot(p.astype(BF16), kv[:, (KVH_A + kh) * DH_A:(KVH_A + kh + 1) * DH_A])
        yield
        o_ref[:, h * DH_A:(h + 1) * DH_A] = o.astype(o_ref.dtype)

    for h0 in range(0, H_A, SWA_GROUP):
        _run_staged([chain(h) for h in range(h0, h0 + SWA_GROUP)])


def swa_prompt(q, kv, sinks, B, S):
    W = WINDOW
    nb = S // W
    return pl.pallas_call(
        _swa_prompt_kernel,
        grid=(B, nb),
        in_specs=[pl.BlockSpec(memory_space=pltpu.SMEM),
                  pl.BlockSpec((W, q.shape[1]), lambda b, j: (b * nb + j, 0)),
                  pl.BlockSpec((W, kv.shape[1]), lambda b, j: (b * nb + jnp.maximum(j - 1, 0), 0)),
                  pl.BlockSpec((W, kv.shape[1]), lambda b, j: (b * nb + j, 0))],
        out_specs=pl.BlockSpec((W, q.shape[1]), lambda b, j: (b * nb + j, 0)),
        out_shape=jax.ShapeDtypeStruct(q.shape, BF16),
        compiler_params=_cparams("parallel", "parallel"),
    )(sinks.reshape(1, H_A), q, kv, kv)


def _swa_sample_kernel(sink_ref, q_ref, kvn_ref, kc_ref, vc_ref, o_ref, ko_ref, vo_ref, *, bb, T):
    W = WINDOW
    R = G_A * T
    row = lax.broadcasted_iota(jnp.int32, (R, 1), 0)
    t_row = row % T
    g_row = row // T
    s_c = lax.broadcasted_iota(jnp.int32, (R, W), 1)
    s_n = lax.broadcasted_iota(jnp.int32, (R, T), 1)
    dist_c = (t_row + W - s_c).astype(F32)
    dist_n = (t_row - s_n).astype(F32)
    valid_c = s_c >= t_row
    valid_n = s_n <= t_row
    nk = KVH_A * DH_A
    slopes, sinks = [], []
    for kh in range(KVH_A):
        slope = jnp.zeros((R, 1), F32)
        sink = jnp.zeros((R, 1), F32)
        for g in range(G_A):
            slope = jnp.where(g_row == g, _alibi_slope(kh * G_A + g, H_A), slope)
            sink = jnp.where(g_row == g, sink_ref[0, kh * G_A + g], sink)
        slopes.append(slope)
        sinks.append(sink)

    def chain(i, kh):
        rows = slice(i * T, (i + 1) * T)
        cols = slice(kh * DH_A, (kh + 1) * DH_A)
        q = q_ref[rows, kh * G_A * DH_A:(kh + 1) * G_A * DH_A]
        qs = jnp.concatenate([q[:, g * DH_A:(g + 1) * DH_A] for g in range(G_A)], axis=0).astype(BF16)
        kn = kvn_ref[rows, cols].astype(BF16)
        vn = kvn_ref[rows, nk + kh * DH_A:nk + (kh + 1) * DH_A].astype(BF16)
        lc = _dot_nt(qs, kc_ref[i, :, cols].astype(BF16))
        ln = _dot_nt(qs, kn)
        yield
        lc = jnp.where(valid_c, lc - slopes[kh] * dist_c, -jnp.inf)
        ln = jnp.where(valid_n, ln - slopes[kh] * dist_n, -jnp.inf)
        pc, pn = _sink_softmax([lc, ln], sinks[kh])
        o = _dot(pc.astype(BF16), vc_ref[i, :, cols].astype(BF16)) + _dot(pn.astype(BF16), vn)
        yield
        for g in range(G_A):
            h = kh * G_A + g
            o_ref[rows, h * DH_A:(h + 1) * DH_A] = o[g * T:(g + 1) * T, :]

    for i in range(bb):
        ko_ref[i, 0:W - T, :] = kc_ref[i, T:, :]
        ko_ref[i, W - T:W, :] = kvn_ref[i * T:(i + 1) * T, :nk]
        vo_ref[i, 0:W - T, :] = vc_ref[i, T:, :]
        vo_ref[i, W - T:W, :] = kvn_ref[i * T:(i + 1) * T, nk:]
    group = 4
    for i0 in range(0, bb, group):
        _run_staged([chain(i, kh) for i in range(i0, min(i0 + group, bb)) for kh in range(KVH_A)])


def swa_sample(q, kvn, k_cache, v_cache, sinks, N, T, *, bb=8):
    W = WINDOW
    nk = KVH_A * DH_A
    rows = lambda width: pl.BlockSpec((bb * T, width), lambda i: (i, 0))
    cache = pl.BlockSpec((bb, W, nk), lambda i: (i, 0, 0))
    return pl.pallas_call(
        functools.partial(_swa_sample_kernel, bb=bb, T=T),
        grid=(N // bb,),
        in_specs=[pl.BlockSpec(memory_space=pltpu.SMEM), rows(q.shape[1]), rows(kvn.shape[1]), cache, cache],
        out_specs=[rows(q.shape[1]), cache, cache],
        out_shape=[jax.ShapeDtypeStruct(q.shape, F32), jax.ShapeDtypeStruct(k_cache.shape, F32),
                   jax.ShapeDtypeStruct(v_cache.shape, F32)],
        compiler_params=_cparams("parallel"),
    )(sinks.reshape(1, H_A), q, kvn, k_cache, v_cache)


MLSTM_GROUP_SHORT = 16


def _run_staged(chains):
    live = list(chains)
    while live:
        still = []
        for ch in live:
            try:
                next(ch)
                still.append(ch)
            except StopIteration:
                pass
        live = still


def _mlstm_head(q, k, v, i_col, f_col, C, n, m, emit):
    c = q.shape[0]
    r = lax.broadcasted_iota(jnp.int32, (c, c), 0)
    s = lax.broadcasted_iota(jnp.int32, (c, c), 1)
    eye = r == s
    causal = s <= r
    lf = jax.nn.log_sigmoid(f_col)
    lf_row = jnp.sum(jnp.where(eye, lf, 0.0), axis=0, keepdims=True)
    i_row = jnp.sum(jnp.where(eye, i_col, 0.0), axis=0, keepdims=True)
    b_col = jnp.sum(jnp.where(causal, lf_row, 0.0), axis=1, keepdims=True)
    b_row = jnp.sum(jnp.where(r <= s, lf, 0.0), axis=0, keepdims=True)
    d = jnp.where(causal, b_col - b_row + i_row, -jnp.inf)
    inter = b_col + m
    mt = jnp.maximum(inter, jnp.max(d, axis=1, keepdims=True))
    qb, kb, vb = q.astype(BF16), k.astype(BF16), v.astype(BF16)
    qk = _dot_nt(qb, kb)
    qC = _dot(qb, C.astype(BF16))
    m_new = mt[c - 1:c, :]
    b_last = b_col[c - 1:c, :]
    w = jnp.exp(b_last - b_col + i_col - m_new)
    decay = jnp.exp(b_last + m - m_new)
    kw = k * w
    kv_update = _dot_tn(kw.astype(BF16), vb)
    yield
    sc = qk * jnp.exp(d - mt)
    g = jnp.exp(inter - mt)
    intra = _dot(sc.astype(BF16), vb)
    yield
    num = intra + g * qC
    den = jnp.sum(sc, axis=1, keepdims=True) + g * jnp.sum(q * n, axis=1, keepdims=True)
    h = num / jnp.maximum(jnp.abs(den), jnp.exp(-mt))
    C_new = decay * C + kv_update
    n_new = decay * n + jnp.sum(kw, axis=0, keepdims=True)
    emit(h, C_new, n_new, m_new)


def _mlstm_kernel(q_ref, k_ref, v_ref, ob_ref, g_ref, C0_ref, n0_ref, m0_ref, y_ref, C_ref, n_ref, m_ref, *, bb, c):
    d = DK_B

    @pl.when(pl.program_id(1) == 0)
    def _():
        C_ref[...] = C0_ref[...]
        n_ref[...] = n0_ref[...]
        m_ref[...] = m0_ref[...]

    q = q_ref[...].astype(F32)
    k = k_ref[...].astype(F32)
    v = v_ref[...].astype(F32)
    gates = g_ref[...]

    def chain(i, h):
        rows = slice(i * c, (i + 1) * c)
        cols = slice(h * d, (h + 1) * d)

        def emit(hh, C_new, n_new, m_new):
            y_ref[rows, cols] = (jax.nn.sigmoid(ob_ref[rows, cols]) * hh).astype(y_ref.dtype)
            C_ref[i, h] = C_new
            n_ref[i, h:h + 1, :] = n_new
            m_ref[i, :, h:h + 1] = m_new

        yield from _mlstm_head(q[rows, cols], k[rows, cols], v[rows, cols], gates[rows, h:h + 1],
                               gates[rows, H_B + h:H_B + h + 1], C_ref[i, h], n_ref[i, h:h + 1, :],
                               m_ref[i, :, h:h + 1], emit)

    pairs = [(i, h) for i in range(bb) for h in range(H_B)]
    group = MLSTM_GROUP_SHORT if c <= 64 else 1
    for c0 in range(0, len(pairs), group):
        _run_staged([chain(i, h) for i, h in pairs[c0:c0 + group]])


def mlstm(q, k, v, ob, gates, C0, n0, m0, N, L, *, bb, out_dtype):
    c = MLSTM_CHUNK if L % MLSTM_CHUNK == 0 else L
    nc = L // c
    d = DK_B
    assert bb == 1 or nc == 1
    rows = lambda width: pl.BlockSpec((bb * c, width), lambda n, j: (n * nc + j, 0))
    st_C = pl.BlockSpec((bb, H_B, d, d), lambda n, j: (n, 0, 0, 0))
    st_n = pl.BlockSpec((bb, H_B, d), lambda n, j: (n, 0, 0))
    st_m = pl.BlockSpec((bb, 1, H_B), lambda n, j: (n, 0, 0))
    y, C, n, m = pl.pallas_call(
        functools.partial(_mlstm_kernel, bb=bb, c=c),
        grid=(N // bb, nc),
        in_specs=[rows(H_B * d), rows(H_B * d), rows(H_B * d), rows(H_B * d), rows(gates.shape[1]), st_C, st_n, st_m],
        out_specs=[rows(H_B * d), st_C, st_n, st_m],
        out_shape=[jax.ShapeDtypeStruct((N * L, H_B * d), out_dtype), jax.ShapeDtypeStruct((N, H_B, d, d), F32),
                   jax.ShapeDtypeStruct((N, H_B, d), F32), jax.ShapeDtypeStruct((N, 1, H_B), F32)],
        compiler_params=_cparams("parallel", "arbitrary"),
    )(q, k, v, ob, gates, C0, n0, m0.reshape(N, 1, H_B))
    return y, C, n, m.reshape(N, H_B)


def _diff_lambda(lq1_ref, lk1_ref, lq2_ref, lk2_ref, lam_init):
    return (jnp.exp(jnp.sum(lq1_ref[...] * lk1_ref[...], axis=-1, keepdims=True))
            - jnp.exp(jnp.sum(lq2_ref[...] * lk2_ref[...], axis=-1, keepdims=True)) + lam_init)


def _diff_head_norm(o, gain, lam_init):
    return _rms(o, gain) * (1.0 - lam_init)


def diff_key_extension(n_groups):
    width = 2 * DH_C
    lane = jnp.arange(n_groups * width) % width
    one = lambda sel: jnp.where(sel, 1.0, 0.0).astype(F32).reshape(1, -1)
    return one((lane == DH_C) | (lane == DH_C + 1)), one(lane == DH_C + 2), one(lane == DH_C + 3)


def _diff_prompt_kernel(qi_tab, ki_tab, slope_ref, q_ref, k_ref, vt_ref, lq1_ref, lk1_ref, lq2_ref, lk2_ref, gain_ref,
                        o_ref, qx_s, m_s, l_s, acc_s, *, tq, tk, lam_init):
    kvh, step = pl.program_id(1), pl.program_id(2)
    qi, ki = qi_tab[step], ki_tab[step]
    dv = 2 * DH_C

    @pl.when(ki == 0)
    def _():
        m_s[...] = jnp.full(m_s.shape, -jnp.inf, F32)
        l_s[...] = jnp.zeros(l_s.shape, F32)
        acc_s[...] = jnp.zeros(acc_s.shape, F32)
        pos_hi, pos_lo = _split_pos(qi * tq + lax.broadcasted_iota(jnp.int32, (tq, DH_C), 0))
        lane = lax.broadcasted_iota(jnp.int32, (tq, DH_C), 1)
        for g in range(G_C):
            slope = slope_ref[0, kvh * G_C + g]
            ext = jnp.where(lane == 0, -slope * pos_hi,
                            jnp.where(lane == 1, -slope * pos_lo, jnp.where(lane <= 3, slope, 0.0))).astype(BF16)
            for mm in range(2):
                qh = q_ref[:, g * dv + mm * DH_C:g * dv + (mm + 1) * DH_C]
                qx_s[g * 2 + mm] = jnp.concatenate([qh, ext], axis=1)

    def block(k0, nk_, triangular):
        k = k_ref[k0:k0 + nk_, :]
        vt = vt_ref[:, k0:k0 + nk_]
        if triangular:
            keep = (lax.broadcasted_iota(jnp.int32, (nk_, tq), 1) >= lax.broadcasted_iota(jnp.int32, (nk_, tq), 0))
        n_heads = 2 * G_C

        def scores(idx):
            mm = idx % 2
            return _dot_nt(k[:, mm * dv:(mm + 1) * dv], qx_s[idx])

        nxt = scores(0)
        for idx in range(n_heads):
            logits = nxt
            if idx + 1 < n_heads:
                nxt = scores(idx + 1)
            if triangular:
                logits = jnp.where(keep, logits, -jnp.inf)
            m_old = m_s[idx]
            m_new = jnp.maximum(m_old, jnp.max(logits, axis=0, keepdims=True))
            alpha = jnp.exp(m_old - m_new)
            p = jnp.exp(logits - m_new)
            l_s[idx] = alpha * l_s[idx] + jnp.sum(p, axis=0, keepdims=True)
            acc_s[idx] = alpha * acc_s[idx] + _dot(vt, p.astype(BF16))
            m_s[idx] = m_new

    n_sub = tk // tq
    diag_sub = qi % n_sub
    on_diagonal = ki == (qi * tq) // tk
    pl.when(jnp.logical_not(on_diagonal))(lambda: block(0, tk, False))
    for j in range(n_sub):
        if j + 1 < n_sub:
            pl.when(on_diagonal & (j < diag_sub))(functools.partial(block, j * tq, tq, False))
        pl.when(on_diagonal & (j == diag_sub))(functools.partial(block, j * tq, tq, True))

    @pl.when(on_diagonal)
    def _():
        lam = _diff_lambda(lq1_ref, lk1_ref, lq2_ref, lk2_ref, lam_init)
        for g in range(G_C):
            ot = acc_s[2 * g] / l_s[2 * g] - lam * (acc_s[2 * g + 1] / l_s[2 * g + 1])
            ot = ot * lax.rsqrt(jnp.mean(ot * ot, axis=0, keepdims=True) + EPS) * gain_ref[...] * (1.0 - lam_init)
            o_ref[:, g * dv:(g + 1) * dv] = ot.T.astype(o_ref.dtype)


def diff_prompt(q, kx, vt, lams, gain, lam_init, B, S, *, tq=512, tk=1024):
    dv = 2 * DH_C
    tk = min(tk, S)
    tq = min(tq, tk)
    assert tk % tq == 0 and S % tk == 0 and math.log2(H_C).is_integer() and H_C <= 8
    nq, nk = S // tq, S // tk
    pairs = [(qi, ki) for qi in range(nq) for ki in range((qi * tq + tq - 1) // tk + 1)]
    qi_tab = jnp.asarray([p[0] for p in pairs], jnp.int32)
    ki_tab = jnp.asarray([p[1] for p in pairs], jnp.int32)
    vec = lambda a: a.reshape(1, -1)
    const = lambda w: pl.BlockSpec((1, w), lambda b, h, s, qt, kt: (0, 0))
    grid_spec = pltpu.PrefetchScalarGridSpec(
        num_scalar_prefetch=2,
        grid=(B, KVH_C, len(pairs)),
        in_specs=[pl.BlockSpec(memory_space=pltpu.SMEM),
                  pl.BlockSpec((tq, G_C * dv), lambda b, h, s, qt, kt: (b * nq + qt[s], h)),
                  pl.BlockSpec((tk, 2 * dv), lambda b, h, s, qt, kt: (b * nk + kt[s], h)),
                  pl.BlockSpec((None, dv, tk), lambda b, h, s, qt, kt: (b, h, kt[s])),
                  const(DH_C), const(DH_C), const(DH_C), const(DH_C),
                  pl.BlockSpec((dv, 1), lambda b, h, s, qt, kt: (0, 0))],
        out_specs=pl.BlockSpec((tq, G_C * dv), lambda b, h, s, qt, kt: (b * nq + qt[s], h)),
        scratch_shapes=[pltpu.VMEM((2 * G_C, tq, dv), BF16), pltpu.VMEM((2 * G_C, 1, tq), F32),
                        pltpu.VMEM((2 * G_C, 1, tq), F32), pltpu.VMEM((2 * G_C, dv, tq), F32)],
    )
    return pl.pallas_call(
        functools.partial(_diff_prompt_kernel, tq=tq, tk=tk, lam_init=lam_init),
        grid_spec=grid_spec,
        out_shape=jax.ShapeDtypeStruct(q.shape, BF16),
        compiler_params=_cparams("parallel", "parallel", "arbitrary"),
    )(qi_tab, ki_tab, jnp.asarray([[_alibi_slope(h, H_C) for h in range(H_C)]], F32), q, kx, vt,
      *[vec(a) for a in lams], gain.reshape(dv, 1))


def _diff_sample_kernel(pt_ref, q_ref, kn_ref, vn_ref, lq1_ref, lk1_ref, lq2_ref, lk2_ref, gain_ref, kt_hbm, v_hbm,
                        o_ref, kbuf, vbuf, sems, qbd_s, m_s, l_s, acc_s, *, pp, T, past, lam_init):
    b, j = pl.program_id(0), pl.program_id(1)
    nb, nj = pl.num_programs(0), pl.num_programs(1)
    dv = 2 * DH_C

    n_slots = kbuf.shape[0]
    ahead = n_slots - 1
    step = b * nj + j
    slot = step % n_slots

    def page_copies(s, sl, read_table=True):
        copies = []
        for p in range(pp):
            page = pt_ref[s // nj, (s % nj) * pp + p] if read_table else 0
            copies.append(pltpu.make_async_copy(kt_hbm.at[page], kbuf.at[sl, p], sems.at[0, sl]))
            copies.append(pltpu.make_async_copy(v_hbm.at[page], vbuf.at[sl, p], sems.at[1, sl]))
        return copies

    @pl.when(step == 0)
    def _():
        for s in range(ahead):
            for cp in page_copies(s, s):
                cp.start()

    @pl.when(step + ahead < nb * nj)
    def _():
        for cp in page_copies(step + ahead, (step + ahead) % n_slots):
            cp.start()

    for cp in page_copies(step, slot, read_table=False):
        cp.wait()
    k_refs = [kbuf.at[slot, p] for p in range(pp)]
    v_refs = [vbuf.at[slot, p] for p in range(pp)]
    RK = 2 * G_C * T
    R = KVH_C * RK
    row = lax.broadcasted_iota(jnp.int32, (R, 1), 0)
    t_row = row % T
    head_row = (row // RK) * G_C + (row // T) % G_C
    slope = jnp.exp2(-8.0 * (head_row + 1).astype(F32) / H_C)

    @pl.when(j == 0)
    def _():
        qbd_s[...] = jnp.zeros(qbd_s.shape, F32)
        for kvh in range(KVH_C):
            for mm in range(2):
                for g in range(G_C):
                    r0 = kvh * RK + (mm * G_C + g) * T
                    c0 = (kvh * 2 + mm) * DH_C
                    q0 = ((kvh * G_C + g) * 2 + mm) * DH_C
                    qbd_s[r0:r0 + T, c0:c0 + DH_C] = q_ref[:, q0:q0 + DH_C]
        m_s[...] = jnp.full(m_s.shape, -jnp.inf, F32)
        l_s[...] = jnp.zeros(l_s.shape, F32)
        acc_s[...] = jnp.zeros(acc_s.shape, F32)

    qbd = qbd_s[...].astype(BF16)

    def update(scores, values, dist, keep):
        logits = scores - slope * dist.astype(F32)
        if keep is not None:
            logits = jnp.where(keep, logits, -jnp.inf)
        m_old = m_s[...]
        m_new = jnp.maximum(m_old, jnp.max(logits, axis=-1, keepdims=True))
        alpha = jnp.exp(m_old - m_new)
        p = jnp.exp(logits - m_new)
        l_s[...] = alpha * l_s[...] + jnp.sum(p, axis=-1, keepdims=True)
        m_s[...] = m_new
        p = p.astype(BF16)
        for kvh in range(KVH_C):
            rows = slice(kvh * RK, (kvh + 1) * RK)
            acc_s[rows, :] = alpha[rows, :] * acc_s[rows, :] + _dot(p[rows, :], values(kvh))

    kb = jnp.concatenate([r[...].astype(BF16) for r in k_refs], axis=1)
    page_values = lambda kvh: jnp.concatenate(
        [r[pl.ds(kvh, PAGE, stride=KVH_C), :].astype(BF16) for r in v_refs], axis=0)
    kpos = j * (pp * PAGE) + lax.broadcasted_iota(jnp.int32, (R, pp * PAGE), 1)
    update(_dot(qbd, kb), page_values, past + t_row - kpos, None)

    @pl.when(j == nj - 1)
    def _():
        pad = jnp.zeros((PAGE - T, kn_ref.shape[1]), F32)
        kn = jnp.concatenate([kn_ref[...], pad], axis=0).astype(BF16)
        vn = jnp.concatenate([vn_ref[...], pad], axis=0).astype(BF16)
        s_new = lax.broadcasted_iota(jnp.int32, (R, PAGE), 1)
        update(_dot_nt(qbd, kn), lambda kvh: vn[:, kvh * dv:(kvh + 1) * dv], t_row - s_new, s_new <= t_row)
        lam = _diff_lambda(lq1_ref, lk1_ref, lq2_ref, lk2_ref, lam_init)
        on = acc_s[...] / l_s[...]
        half = G_C * T
        for kvh in range(KVH_C):
            o = on[kvh * RK:kvh * RK + half, :] - lam * on[kvh * RK + half:(kvh + 1) * RK, :]
            o = _diff_head_norm(o, gain_ref[...], lam_init)
            for g in range(G_C):
                h = kvh * G_C + g
                o_ref[:, h * dv:(h + 1) * dv] = o[g * T:(g + 1) * T, :]


def diff_sample(q, kn, vn, cache_kt, cache_v, page_table, lams, gain, lam_init, N, T, *, pp=16, n_slots=3):
    dv = 2 * DH_C
    n_pages = page_table.shape[1]
    pp = math.gcd(pp, n_pages)
    R = KVH_C * 2 * G_C * T
    width = KVH_C * 2 * DH_C
    assert cache_kt.shape[1:] == (width, PAGE) and cache_v.shape[1:] == (PAGE * KVH_C, dv)
    assert N * (n_pages // pp) >= n_slots - 1
    vec = lambda a: a.reshape(1, -1)
    rows = lambda w: pl.BlockSpec((T, w), lambda b, j, pt: (b, 0))
    const = lambda w: pl.BlockSpec((1, w), lambda b, j, pt: (0, 0))
    in_hbm = pl.BlockSpec(memory_space=pl.ANY)
    grid_spec = pltpu.PrefetchScalarGridSpec(
        num_scalar_prefetch=1,
        grid=(N, n_pages // pp),
        in_specs=[rows(q.shape[1]), rows(width), rows(width), const(DH_C), const(DH_C), const(DH_C), const(DH_C),
                  const(dv), in_hbm, in_hbm],
        out_specs=rows(q.shape[1]),
        scratch_shapes=[pltpu.VMEM((n_slots, pp, width, PAGE), F32), pltpu.VMEM((n_slots, pp, PAGE * KVH_C, dv), F32),
                        pltpu.SemaphoreType.DMA((2, n_slots)),
                        pltpu.VMEM((R, width), F32), pltpu.VMEM((R, 1), F32), pltpu.VMEM((R, 1), F32),
                        pltpu.VMEM((R, dv), F32)],
    )
    return pl.pallas_call(
        functools.partial(_diff_sample_kernel, pp=pp, T=T, past=n_pages * PAGE, lam_init=lam_init),
        grid_spec=grid_spec,
        out_shape=jax.ShapeDtypeStruct(q.shape, F32),
        compiler_params=_cparams("arbitrary", "arbitrary"),
    )(page_table, q, kn, vn, *[vec(a) for a in lams], vec(gain), cache_kt, cache_v)


LANES = 128


def _cross_kernel(q_ref, mk_ref, mv_ref, o_ref, *, bb, tq, interleaved):
    D = q_ref.shape[1]
    nch = D // XH // LANES
    if interleaved:
        M = mk_ref.shape[1] // (XH * nch)
        chunk = lambda ref, i, h, c: ref[i, pl.ds(c * XH + h, M, stride=XH * nch), :]
    else:
        chunk = lambda ref, i, h, c: ref[i, :, (h * nch + c) * LANES:(h * nch + c + 1) * LANES]
    col = lambda h, c: slice((h * nch + c) * LANES, (h * nch + c + 1) * LANES)
    rows = lambda i: slice(i * tq, (i + 1) * tq)

    def scores(i, h):
        q = q_ref[rows(i), h * nch * LANES:(h + 1) * nch * LANES].astype(BF16)
        return sum(_dot_nt(q[:, c * LANES:(c + 1) * LANES], chunk(mk_ref, i, h, c).astype(BF16)) for c in range(nch))

    def probs(s):
        e = jnp.exp(s - jnp.max(s, axis=-1, keepdims=True))
        return (e * (1.0 / jnp.sum(e, axis=-1, keepdims=True))).astype(BF16)

    def output(i, h, p):
        for c in range(nch):
            o_ref[rows(i), col(h, c)] = _dot(p, chunk(mv_ref, i, h, c).astype(BF16)).astype(o_ref.dtype)

    pairs = [(i, h) for i in range(bb) for h in range(XH)]
    group = 2 * XH if tq <= 64 else 1
    for g0 in range(0, len(pairs), group):
        grp = pairs[g0:g0 + group]
        ss = [scores(i, h) for i, h in grp]
        ps = [probs(s) for s in ss]
        for (i, h), p in zip(grp, ps):
            output(i, h, p)


def interleave_heads(mem):
    depth, N, M, XH_, xdh = mem.shape
    nch = xdh // LANES
    return jnp.transpose(mem.reshape(depth, N, M, XH_, nch, LANES), (0, 1, 2, 4, 3, 5)).reshape(
        depth, N, M * XH_ * nch, LANES)


def cross_core(q, mk, mv, N, L, *, bb, tq, out_dtype, layer=None):
    D = q.shape[1]
    nt = L // tq
    assert bb == 1 or nt == 1
    if layer is None:
        mem = pl.BlockSpec((bb,) + mk.shape[1:], lambda n, i: (n, 0, 0))
    else:
        mem = pl.BlockSpec((None, bb) + mk.shape[2:], lambda n, i: (layer, n, 0, 0))
    return pl.pallas_call(
        functools.partial(_cross_kernel, bb=bb, tq=tq, interleaved=layer is not None),
        grid=(N // bb, nt),
        in_specs=[pl.BlockSpec((bb * tq, D), lambda n, i: (n * nt + i, 0)), mem, mem],
        out_specs=pl.BlockSpec((bb * tq, D), lambda n, i: (n * nt + i, 0)),
        out_shape=jax.ShapeDtypeStruct(q.shape, out_dtype),
        compiler_params=_cparams("parallel", "parallel"),
    )(q, mk, mv)


def _lambda_init(layer):
    return 0.8 - 0.6 * math.exp(-0.3 * layer)


def _ab_segments(act_dtype):
    qa, kv, hb = H_A * DH_A, 2 * KVH_A * DH_A, H_B * DK_B
    o = 0
    segs = []
    for width, dtype, scale in ((qa, act_dtype, DH_A ** -0.5), (kv, F32, 1.0), (hb, act_dtype, 1.0),
                                (hb, act_dtype, DK_B ** -0.5), (hb, act_dtype, 1.0), (hb, F32, 1.0)):
        segs.append(Seg(o, width, dtype, scale))
        o += width
    segs.append(Seg(o, 128, F32, 1.0, True))
    return segs, o + 128


def kernel(x_prompt, x_sample, mem_prompt, cache_swa_k, cache_swa_v, state_mlstm_C, state_mlstm_n, state_mlstm_m, cache_diff_k, cache_diff_v, page_table, cache_mem_k, cache_mem_v, norm_mix, norm_cross, norm_ffn, norm_final, w_in_ab, b_mlstm_i, b_mlstm_f, attn_sinks, w_out_ab, w_in_c, lambda_q1, lambda_k1, lambda_q2, lambda_k2, diff_norm, w_out_c, w_xq, w_xk, w_xv, w_xo, w_gate, w_up, w_down):
    B, S, D = x_prompt.shape
    NS, TS, _ = x_sample.shape
    M = mem_prompt.shape[1]
    depth = norm_mix.shape[0]
    TMP = 512
    TMS = min(512, NS * TS)
    xdh = D // XH
    bf = lambda a: a.astype(BF16)

    xp = x_prompt.reshape(B * S, D)
    xs = x_sample.reshape(NS * TS, D)
    mem = mem_prompt.reshape(B * M, D)
    zero_bias = lambda n: jnp.zeros((1, n), F32)

    outs = {k: [] for k in ("swa_kp", "swa_vp", "swa_ks", "swa_vs", "C_p", "n_p", "m_p", "C_s", "n_s", "m_s",
                            "dk_p", "dv_p", "dk_s", "dv_s", "mk_p", "mv_p")}
    for layer in range(depth):
        if layer % 2 == 0:
            e = layer // 2
            segs_p, n_cols = _ab_segments(BF16)
            segs_s, _ = _ab_segments(F32)
            w_in = bf(jnp.pad(w_in_ab[e], ((0, 0), (0, n_cols - w_in_ab.shape[2]))))
            bias = jnp.zeros((1, n_cols), F32).at[0, n_cols - 128:n_cols - 128 + 2 * H_B].set(
                jnp.concatenate([b_mlstm_i[e], b_mlstm_f[e]]))
            nqa = H_A * DH_A
            w_out_a, w_out_b = bf(w_out_ab[e, :nqa]), bf(w_out_ab[e, nqa:])
            nk = KVH_A * DH_A

            qa, kva, qb, kb, vb, ob, gates = norm_matmul(xp, norm_mix[layer], w_in, bias, segs_p, tm=TMP)
            ya = swa_prompt(qa, kva, attn_sinks[e], B, S)
            yb, C, n, m = mlstm(qb, kb, vb, ob, gates, jnp.zeros((B, H_B, DK_B, DK_B), F32),
                                jnp.zeros((B, H_B, DK_B), F32), jnp.zeros((B, H_B), F32), B, S, bb=1, out_dtype=BF16)
            mix_p = ([ya, yb], [w_out_a, w_out_b])
            kv_last = kva.reshape(B, S, 2 * nk)[:, S - WINDOW:]
            outs["swa_kp"].append(kv_last[..., :nk].reshape(B, WINDOW, KVH_A, DH_A))
            outs["swa_vp"].append(kv_last[..., nk:].reshape(B, WINDOW, KVH_A, DH_A))
            outs["C_p"].append(C); outs["n_p"].append(n); outs["m_p"].append(m)

            qa, kva, qb, kb, vb, ob, gates = norm_matmul(xs, norm_mix[layer], w_in, bias, segs_s, tm=TMS)
            ya, k_new, v_new = swa_sample(qa, kva, cache_swa_k[e].reshape(NS, WINDOW, nk),
                                          cache_swa_v[e].reshape(NS, WINDOW, nk), attn_sinks[e], NS, TS)
            yb, C, n, m = mlstm(qb, kb, vb, ob, gates, state_mlstm_C[e], state_mlstm_n[e], state_mlstm_m[e], NS, TS,
                                bb=8, out_dtype=F32)
            xs = matmul_residual([ya, yb], [w_out_a, w_out_b], xs, tm=TMS)
            outs["swa_ks"].append(k_new.reshape(NS, WINDOW, KVH_A, DH_A))
            outs["swa_vs"].append(v_new.reshape(NS, WINDOW, KVH_A, DH_A))
            outs["C_s"].append(C); outs["n_s"].append(n); outs["m_s"].append(m)
        else:
            c = layer // 2
            lam_init = _lambda_init(layer)
            lams = (lambda_q1[c], lambda_k1[c], lambda_q2[c], lambda_k2[c])
            nq, nkv = H_C * 2 * DH_C, KVH_C * 2 * DH_C
            w_in = bf(w_in_c[c])
            w_out = bf(w_out_c[c])

            n_groups = KVH_C * 2
            w_kx = jnp.pad(w_in[:, nq:nq + nkv].reshape(D, n_groups, DH_C), ((0, 0), (0, 0), (0, DH_C)))
            w_px = jnp.concatenate([w_in[:, :nq], w_kx.reshape(D, 2 * nkv), w_in[:, nq + nkv:]], axis=1)
            ones, u_hi, u_lo = diff_key_extension(n_groups)
            widen = lambda a: jnp.pad(a, ((0, 0), (nq, nkv)))
            o_kx, o_v = nq, nq + 2 * nkv
            segs = [Seg(0, nq, BF16, DH_C ** -0.5), Seg(o_kx, 2 * nkv, BF16, 1.0, True, True), Seg(o_v, nkv, F32)]
            q, kx, v, kt, vt = norm_matmul(
                xp, norm_mix[layer], w_px, widen(ones), segs, tm=TMP, pos=(widen(u_hi), widen(u_lo), S),
                w_t=[(w_in[:, nq:nq + nkv].T, F32), (w_in[:, nq + nkv:].T, BF16)])
            o = diff_prompt(q, kx, vt, lams, diff_norm[c], lam_init, B, S)
            mix_p = ([o], [w_out])
            outs["dk_p"].append(jnp.transpose(kt.reshape(B, KVH_C, 2, DH_C, S), (0, 4, 1, 2, 3)))
            outs["dv_p"].append(v.reshape(B, S, KVH_C, 2 * DH_C))

            segs = [Seg(0, nq, F32, DH_C ** -0.5), Seg(nq, nkv, F32), Seg(nq + nkv, nkv, F32)]
            q, k, v = norm_matmul(xs, norm_mix[layer], w_in, zero_bias(nq + 2 * nkv), segs, tm=TMS)
            n_pool = cache_diff_k.shape[1]
            cache_kt = jnp.transpose(cache_diff_k[c], (0, 2, 3, 4, 1)).reshape(n_pool, nkv, PAGE)
            cache_vr = cache_diff_v[c].reshape(n_pool, PAGE * KVH_C, 2 * DH_C)
            o = diff_sample(q, k, v, cache_kt, cache_vr, page_table, lams, diff_norm[c], lam_init, NS, TS)
            xs = matmul_residual([o], [w_out], xs, tm=TMS)
            outs["dk_s"].append(k.reshape(NS, TS, KVH_C, 2, DH_C))
            outs["dv_s"].append(v.reshape(NS, TS, KVH_C, 2 * DH_C))

        w_kv = bf(jnp.concatenate([w_xk[layer], w_xv[layer]], axis=1))
        segs = [Seg(0, D, F32), Seg(D, D, F32), Seg(0, D, BF16), Seg(D, D, BF16)]
        mk, mv, mk16, mv16 = norm_matmul(mem, jnp.ones((D,), F32), w_kv, zero_bias(2 * D), segs, tm=TMP, norm=False)
        outs["mk_p"].append(mk.reshape(B, M, XH, xdh))
        outs["mv_p"].append(mv.reshape(B, M, XH, xdh))
        w_q, w_o = bf(w_xq[layer]), bf(w_xo[layer])
        last = layer == depth - 1
        wg, wu, wd = bf(w_gate[layer]), bf(w_up[layer]), bf(w_down[layer])

        xp = post_mixer(*mix_p, xp, norm_cross[layer], w_q, mk16.reshape(B, M, D), mv16.reshape(B, M, D), w_o,
                        norm_ffn[layer], wg, wu, wd, norm_final, B, S, tq=TMP, final_norm=last)

        (q,) = norm_matmul(xs, norm_cross[layer], w_q, zero_bias(D), [Seg(0, D, F32, xdh ** -0.5)], tm=TMS)
        o = cross_core(q, interleave_heads(cache_mem_k), interleave_heads(cache_mem_v), NS, TS, bb=8, tq=TS,
                       out_dtype=F32, layer=layer)
        xs = matmul_residual([o], [w_o], xs, tm=TMS)
        xs = ffn(xs, norm_ffn[layer], wg, wu, wd, norm_final, tm=TMS, final_norm=last)

    st = jnp.stack
    return (xp.reshape(B, S, D), xs.reshape(NS, TS, D),
            st(outs["swa_kp"]), st(outs["swa_vp"]), st(outs["swa_ks"]), st(outs["swa_vs"]),
            st(outs["C_p"]), st(outs["n_p"]), st(outs["m_p"]), st(outs["C_s"]), st(outs["n_s"]), st(outs["m_s"]),
            st(outs["dk_p"]), st(outs["dv_p"]), st(outs["dk_s"]), st(outs["dv_s"]),
            st(outs["mk_p"]), st(outs["mv_p"]))
```

```python
import functools
import math
from typing import NamedTuple

import jax
import jax.numpy as jnp
from jax import lax
from jax.experimental import pallas as pl
from jax.experimental.pallas import tpu as pltpu

F32 = jnp.float32
BF16 = jnp.bfloat16
EPS = 1e-6

WINDOW = 128
DH_A, KVH_A, G_A = 64, 2, 4
H_A = KVH_A * G_A
H_B, DK_B = 4, 128
MLSTM_CHUNK = 128
DH_C, KVH_C, G_C = 64, 2, 4
H_C = KVH_C * G_C
XH = 4
PAGE = 128

VMEM_LIMIT = 56 * 1024 * 1024


def _alibi_slope(h, n):
    return 2.0 ** (-8.0 * (h + 1) / n)


def _cparams(*sem):
    return pltpu.CompilerParams(dimension_semantics=sem, vmem_limit_bytes=VMEM_LIMIT)


def _resident(shape):
    return pl.BlockSpec(shape, lambda *_: (0,) * len(shape), pipeline_mode=pl.Buffered(1))


def _dot(a, b):
    return jnp.dot(a, b, preferred_element_type=F32)


def _dot_nt(a, b):
    return lax.dot_general(a, b, (((1,), (1,)), ((), ())), preferred_element_type=F32)


def _dot_tn(a, b):
    return lax.dot_general(a, b, (((0,), (0,)), ((), ())), preferred_element_type=F32)


def _rms(x, g):
    return x * lax.rsqrt(jnp.mean(x * x, axis=-1, keepdims=True) + EPS) * g


class Seg(NamedTuple):
    start: int
    width: int
    dtype: object
    scale: float = 1.0
    bias: bool = False
    pos: bool = False


POS_SPLIT = 256


def _split_pos(pos):
    return (pos // POS_SPLIT * POS_SPLIT).astype(F32), (pos % POS_SPLIT).astype(F32)


def _norm_matmul_kernel(x_ref, g_ref, w_ref, b_ref, uh_ref, ul_ref, *rest, segs, norm, seq_len, n_t):
    x = x_ref[...]
    tm = x.shape[0]
    h = (_rms(x, g_ref[...]) if norm else x).astype(BF16)
    wt_refs, out_refs, ot_refs = rest[:n_t], rest[n_t:len(rest) - n_t], rest[len(rest) - n_t:]
    for wt_ref, ot_ref in zip(wt_refs, ot_refs):
        ot_ref[...] = _dot_nt(wt_ref[...], h).astype(ot_ref.dtype)
    for o_ref, s in zip(out_refs, segs):
        cols = slice(s.start, s.start + s.width)
        z = _dot(h, w_ref[:, cols])
        if s.bias:
            z = z + b_ref[:, cols]
        if s.pos:
            row = pl.program_id(0) * tm + lax.broadcasted_iota(jnp.int32, (tm, 1), 0)
            pos_hi, pos_lo = _split_pos(row % seq_len)
            z = z + pos_hi * uh_ref[:, cols] + pos_lo * ul_ref[:, cols]
        if s.scale != 1.0:
            z = z * s.scale
        o_ref[...] = z.astype(o_ref.dtype)


def norm_matmul(x, gain, w, bias, segs, *, tm, norm=True, pos=None, w_t=()):
    T, D = x.shape
    N = w.shape[1]
    assert T % tm == 0
    u_hi, u_lo, seq_len = pos if pos is not None else (jnp.zeros((1, N), F32), jnp.zeros((1, N), F32), T)
    in_specs = [pl.BlockSpec((tm, D), lambda i: (i, 0)), _resident((1, D)), _resident((D, N)), _resident((1, N)),
                _resident((1, N)), _resident((1, N))]
    out_specs = [pl.BlockSpec((tm, s.width), lambda i: (i, 0)) for s in segs]
    out_shape = [jax.ShapeDtypeStruct((T, s.width), s.dtype) for s in segs]
    args = [x, gain.reshape(1, D), w, bias, u_hi, u_lo]
    assert not w_t or seq_len % tm == 0
    nt = seq_len // tm
    for wt, dtype in w_t:
        in_specs.append(_resident(wt.shape))
        out_specs.append(pl.BlockSpec((None, wt.shape[0], tm), lambda i: (i // nt, 0, i % nt)))
        out_shape.append(jax.ShapeDtypeStruct((T // seq_len, wt.shape[0], seq_len), dtype))
        args.append(wt)
    return pl.pallas_call(
        functools.partial(_norm_matmul_kernel, segs=tuple(segs), norm=norm, seq_len=seq_len, n_t=len(w_t)),
        grid=(T // tm,),
        in_specs=in_specs, out_specs=out_specs, out_shape=out_shape,
        compiler_params=_cparams("parallel"),
    )(*args)


def _matmul_residual_kernel(*refs, n_in):
    a_refs, w_refs, x_ref, o_ref = refs[:n_in], refs[n_in:2 * n_in], refs[2 * n_in], refs[2 * n_in + 1]
    acc = x_ref[...]
    for a_ref, w_ref in zip(a_refs, w_refs):
        acc = acc + _dot(a_ref[...].astype(BF16), w_ref[...])
    o_ref[...] = acc


def matmul_residual(acts, ws, x, *, tm):
    T, D = x.shape
    n_in = len(acts)
    return pl.pallas_call(
        functools.partial(_matmul_residual_kernel, n_in=n_in),
        grid=(T // tm,),
        in_specs=([pl.BlockSpec((tm, a.shape[1]), lambda i: (i, 0)) for a in acts]
                  + [_resident(w.shape) for w in ws] + [pl.BlockSpec((tm, D), lambda i: (i, 0))]),
        out_specs=pl.BlockSpec((tm, D), lambda i: (i, 0)),
        out_shape=jax.ShapeDtypeStruct((T, D), F32),
        compiler_params=_cparams("parallel"),
    )(*acts, *ws, x)


def _ffn_kernel(x_ref, g_ref, wg_ref, wu_ref, wd_ref, gf_ref, o_ref, *, ff_chunk, final_norm):
    x = x_ref[...]
    h = _rms(x, g_ref[...]).astype(BF16)
    acc = x
    for c in range(0, wg_ref.shape[1], ff_chunk):
        gate = _dot(h, wg_ref[:, c:c + ff_chunk])
        up = _dot(h, wu_ref[:, c:c + ff_chunk])
        a = gate * jax.nn.sigmoid(gate) * up
        acc = acc + _dot(a.astype(BF16), wd_ref[c:c + ff_chunk, :])
    o_ref[...] = _rms(acc, gf_ref[...]) if final_norm else acc


def ffn(x, gain, wg, wu, wd, gain_final, *, tm, final_norm, ff_chunk=256):
    T, D = x.shape
    F = wg.shape[1]
    assert F % ff_chunk == 0
    return pl.pallas_call(
        functools.partial(_ffn_kernel, ff_chunk=ff_chunk, final_norm=final_norm),
        grid=(T // tm,),
        in_specs=[pl.BlockSpec((tm, D), lambda i: (i, 0)), _resident((1, D)), _resident((D, F)), _resident((D, F)),
                  _resident((F, D)), _resident((1, D))],
        out_specs=pl.BlockSpec((tm, D), lambda i: (i, 0)),
        out_shape=jax.ShapeDtypeStruct((T, D), F32),
        compiler_params=_cparams("parallel"),
    )(x, gain.reshape(1, D), wg, wu, wd, gain_final.reshape(1, D))


def _post_mixer_kernel(*refs, n_in, ff_chunk, final_norm):
    a_refs, w_refs = refs[:n_in], refs[n_in:2 * n_in]
    (x_ref, gc_ref, wq_ref, mk_ref, mv_ref, wo_ref, gf_ref, wg_ref, wu_ref, wd_ref, gl_ref, o_ref, att_s) = refs[2 * n_in:]
    x = x_ref[...]
    for a_ref, w_ref in zip(a_refs, w_refs):
        x = x + _dot(a_ref[...].astype(BF16), w_ref[...])
    xdh = x.shape[1] // XH
    h = _rms(x, gc_ref[...]).astype(BF16)
    q = (_dot(h, wq_ref[...]) * xdh ** -0.5).astype(BF16)
    cols = lambda hh: slice(hh * xdh, (hh + 1) * xdh)
    scores = lambda hh: _dot_nt(q[:, cols(hh)], mk_ref[0, :, cols(hh)])
    nxt = scores(0)
    for hh in range(XH):
        s = nxt
        if hh + 1 < XH:
            nxt = scores(hh + 1)
        e = jnp.exp(s - jnp.max(s, axis=-1, keepdims=True))
        p = (e * (1.0 / jnp.sum(e, axis=-1, keepdims=True))).astype(BF16)
        att_s[:, cols(hh)] = _dot(p, mv_ref[0, :, cols(hh)]).astype(BF16)
    x = x + _dot(att_s[...], wo_ref[...])
    h = _rms(x, gf_ref[...]).astype(BF16)
    acc = x
    for c in range(0, wg_ref.shape[1], ff_chunk):
        gate = _dot(h, wg_ref[:, c:c + ff_chunk])
        up = _dot(h, wu_ref[:, c:c + ff_chunk])
        acc = acc + _dot((gate * jax.nn.sigmoid(gate) * up).astype(BF16), wd_ref[c:c + ff_chunk, :])
    o_ref[...] = _rms(acc, gl_ref[...]) if final_norm else acc


def post_mixer(acts, w_outs, x, gain_cross, w_q, mk, mv, w_o, gain_ffn, wg, wu, wd, gain_final, N, L, *, tq, final_norm,
               ff_chunk=256):
    T, D = x.shape
    F = wg.shape[1]
    nt = L // tq
    n_in = len(acts)
    assert L % tq == 0 and F % ff_chunk == 0
    rows = lambda w: pl.BlockSpec((tq, w), lambda n, i: (n * nt + i, 0))
    mem = pl.BlockSpec((1,) + mk.shape[1:], lambda n, i: (n, 0, 0))
    vec = lambda g: g.reshape(1, D)
    return pl.pallas_call(
        functools.partial(_post_mixer_kernel, n_in=n_in, ff_chunk=ff_chunk, final_norm=final_norm),
        grid=(N, nt),
        in_specs=([rows(a.shape[1]) for a in acts] + [_resident(w.shape) for w in w_outs]
                  + [rows(D), _resident((1, D)), _resident((D, D)), mem, mem, _resident((D, D)), _resident((1, D)),
                     _resident((D, F)), _resident((D, F)), _resident((F, D)), _resident((1, D))]),
        out_specs=rows(D),
        out_shape=jax.ShapeDtypeStruct((T, D), F32),
        scratch_shapes=[pltpu.VMEM((tq, D), BF16)],
        compiler_params=_cparams("parallel", "parallel"),
    )(*acts, *w_outs, x, vec(gain_cross), w_q, mk, mv, w_o, vec(gain_ffn), wg, wu, wd, vec(gain_final))


def _sink_softmax(pieces, sink):
    m = sink
    for lg in pieces:
        m = jnp.maximum(m, jnp.max(lg, axis=-1, keepdims=True))
    es = [jnp.exp(lg - m) for lg in pieces]
    den = jnp.exp(sink - m)
    for e in es:
        den = den + jnp.sum(e, axis=-1, keepdims=True)
    inv = 1.0 / den
    return [e * inv for e in es]


SWA_GROUP = 8


def _swa_prompt_kernel(sink_ref, q_ref, kvp_ref, kvo_ref, o_ref, *, nsb):
    W = WINDOW
    j = pl.program_id(1)
    t = lax.broadcasted_iota(jnp.int32, (W, 2 * W), 0)
    s = lax.broadcasted_iota(jnp.int32, (W, 2 * W), 1)
    dist = t + W - s
    band = (dist >= 0) & (dist <= W)
    distf = dist.astype(F32)
    for sb in range(nsb):
        rows = slice(sb * W, (sb + 1) * W)
        q = q_ref[rows, :]
        if sb == 0:
            kv = jnp.concatenate([kvp_ref[...], kvo_ref[0:W, :]], axis=0).astype(BF16)
            valid = band & ((s >= W) | (j > 0))
        else:
            kv = kvo_ref[(sb - 1) * W:(sb + 1) * W, :].astype(BF16)
            valid = band

        def chain(h, q=q, kv=kv, valid=valid, rows=rows):
            kh = h // G_A
            sc = _dot_nt(q[:, h * DH_A:(h + 1) * DH_A], kv[:, kh * DH_A:(kh + 1) * DH_A])
            yield
            logits = jnp.where(valid, sc - _alibi_slope(h, H_A) * distf, -jnp.inf)
            (p,) = _sink_softmax([logits], sink_ref[0, h])
            o = _dot(p.astype(BF16), kv[:, (KVH_A + kh) * DH_A:(KVH_A + kh + 1) * DH_A])
            yield
            o_ref[rows, h * DH_A:(h + 1) * DH_A] = o.astype(o_ref.dtype)

        for h0 in range(0, H_A, SWA_GROUP):
            _run_staged([chain(h) for h in range(h0, h0 + SWA_GROUP)])


def swa_prompt(q, kv, sinks, B, S, *, nsb=4):
    W = WINDOW
    nb = S // W
    nsb = math.gcd(nsb, nb)
    ng = nb // nsb
    big = lambda width: pl.BlockSpec((W * nsb, width), lambda b, j: (b * ng + j, 0))
    return pl.pallas_call(
        functools.partial(_swa_prompt_kernel, nsb=nsb),
        grid=(B, ng),
        in_specs=[pl.BlockSpec(memory_space=pltpu.SMEM), big(q.shape[1]),
                  pl.BlockSpec((W, kv.shape[1]), lambda b, j: (b * nb + jnp.maximum(j * nsb - 1, 0), 0)),
                  big(kv.shape[1])],
        out_specs=big(q.shape[1]),
        out_shape=jax.ShapeDtypeStruct(q.shape, BF16),
        compiler_params=_cparams("parallel", "parallel"),
    )(sinks.reshape(1, H_A), q, kv, kv)


def _swa_sample_kernel(sink_ref, q_ref, kvn_ref, kc_ref, vc_ref, o_ref, ko_ref, vo_ref, *, bb, T):
    W = WINDOW
    R = G_A * T
    row = lax.broadcasted_iota(jnp.int32, (R, 1), 0)
    t_row = row % T
    g_row = row // T
    s_c = lax.broadcasted_iota(jnp.int32, (R, W), 1)
    s_n = lax.broadcasted_iota(jnp.int32, (R, T), 1)
    dist_c = (t_row + W - s_c).astype(F32)
    dist_n = (t_row - s_n).astype(F32)
    valid_c = s_c >= t_row
    valid_n = s_n <= t_row
    nk = KVH_A * DH_A
    slopes, sinks = [], []
    for kh in range(KVH_A):
        slope = jnp.zeros((R, 1), F32)
        sink = jnp.zeros((R, 1), F32)
        for g in range(G_A):
            slope = jnp.where(g_row == g, _alibi_slope(kh * G_A + g, H_A), slope)
            sink = jnp.where(g_row == g, sink_ref[0, kh * G_A + g], sink)
        slopes.append(slope)
        sinks.append(sink)

    def chain(i, kh):
        rows = slice(i * T, (i + 1) * T)
        cols = slice(kh * DH_A, (kh + 1) * DH_A)
        q = q_ref[rows, kh * G_A * DH_A:(kh + 1) * G_A * DH_A]
        qs = jnp.concatenate([q[:, g * DH_A:(g + 1) * DH_A] for g in range(G_A)], axis=0).astype(BF16)
        kn = kvn_ref[rows, cols].astype(BF16)
        vn = kvn_ref[rows, nk + kh * DH_A:nk + (kh + 1) * DH_A].astype(BF16)
        lc = _dot_nt(qs, kc_ref[i, :, cols].astype(BF16))
        ln = _dot_nt(qs, kn)
        yield
        lc = jnp.where(valid_c, lc - slopes[kh] * dist_c, -jnp.inf)
        ln = jnp.where(valid_n, ln - slopes[kh] * dist_n, -jnp.inf)
        pc, pn = _sink_softmax([lc, ln], sinks[kh])
        o = _dot(pc.astype(BF16), vc_ref[i, :, cols].astype(BF16)) + _dot(pn.astype(BF16), vn)
        yield
        for g in range(G_A):
            h = kh * G_A + g
            o_ref[rows, h * DH_A:(h + 1) * DH_A] = o[g * T:(g + 1) * T, :]

    for i in range(bb):
        ko_ref[i, 0:W - T, :] = kc_ref[i, T:, :]
        ko_ref[i, W - T:W, :] = kvn_ref[i * T:(i + 1) * T, :nk]
        vo_ref[i, 0:W - T, :] = vc_ref[i, T:, :]
        vo_ref[i, W - T:W, :] = kvn_ref[i * T:(i + 1) * T, nk:]
    group = 4
    for i0 in range(0, bb, group):
        _run_staged([chain(i, kh) for i in range(i0, min(i0 + group, bb)) for kh in range(KVH_A)])


def swa_sample(q, kvn, k_cache, v_cache, sinks, N, T, *, bb=8):
    W = WINDOW
    nk = KVH_A * DH_A
    rows = lambda width: pl.BlockSpec((bb * T, width), lambda i: (i, 0))
    cache = pl.BlockSpec((bb, W, nk), lambda i: (i, 0, 0))
    return pl.pallas_call(
        functools.partial(_swa_sample_kernel, bb=bb, T=T),
        grid=(N // bb,),
        in_specs=[pl.BlockSpec(memory_space=pltpu.SMEM), rows(q.shape[1]), rows(kvn.shape[1]), cache, cache],
        out_specs=[rows(q.shape[1]), cache, cache],
        out_shape=[jax.ShapeDtypeStruct(q.shape, F32), jax.ShapeDtypeStruct(k_cache.shape, F32),
                   jax.ShapeDtypeStruct(v_cache.shape, F32)],
        compiler_params=_cparams("parallel"),
    )(sinks.reshape(1, H_A), q, kvn, k_cache, v_cache)


MLSTM_GROUP_SHORT = 16


def _run_staged(chains):
    live = list(chains)
    while live:
        still = []
        for ch in live:
            try:
                next(ch)
                still.append(ch)
            except StopIteration:
                pass
        live = still


def _mlstm_head(q, k, v, i_col, f_col, C, n, m, emit):
    c = q.shape[0]
    r = lax.broadcasted_iota(jnp.int32, (c, c), 0)
    s = lax.broadcasted_iota(jnp.int32, (c, c), 1)
    eye = r == s
    causal = s <= r
    lf = jax.nn.log_sigmoid(f_col)
    lf_row = jnp.sum(jnp.where(eye, lf, 0.0), axis=0, keepdims=True)
    i_row = jnp.sum(jnp.where(eye, i_col, 0.0), axis=0, keepdims=True)
    b_col = jnp.sum(jnp.where(causal, lf_row, 0.0), axis=1, keepdims=True)
    b_row = jnp.sum(jnp.where(r <= s, lf, 0.0), axis=0, keepdims=True)
    d = jnp.where(causal, b_col - b_row + i_row, -jnp.inf)
    inter = b_col + m
    mt = jnp.maximum(inter, jnp.max(d, axis=1, keepdims=True))
    qb, kb, vb = q.astype(BF16), k.astype(BF16), v.astype(BF16)
    qk = _dot_nt(qb, kb)
    qC = _dot(qb, C.astype(BF16))
    m_new = mt[c - 1:c, :]
    b_last = b_col[c - 1:c, :]
    w = jnp.exp(b_last - b_col + i_col - m_new)
    decay = jnp.exp(b_last + m - m_new)
    kw = k * w
    kv_update = _dot_tn(kw.astype(BF16), vb)
    yield
    sc = qk * jnp.exp(d - mt)
    g = jnp.exp(inter - mt)
    intra = _dot(sc.astype(BF16), vb)
    yield
    num = intra + g * qC
    den = jnp.sum(sc, axis=1, keepdims=True) + g * jnp.sum(q * n, axis=1, keepdims=True)
    h = num / jnp.maximum(jnp.abs(den), jnp.exp(-mt))
    C_new = decay * C + kv_update
    n_new = decay * n + jnp.sum(kw, axis=0, keepdims=True)
    emit(h, C_new, n_new, m_new)


def _mlstm_kernel(q_ref, k_ref, v_ref, ob_ref, g_ref, C0_ref, n0_ref, m0_ref, y_ref, C_ref, n_ref, m_ref, *, bb, c, cps):
    d = DK_B

    @pl.when(pl.program_id(1) == 0)
    def _():
        C_ref[...] = C0_ref[...]
        n_ref[...] = n0_ref[...]
        m_ref[...] = m0_ref[...]

    if cps == 1:
        q = q_ref[...].astype(F32)
        k = k_ref[...].astype(F32)
        v = v_ref[...].astype(F32)
        gates = g_ref[...]

    def chain(i, h, blk):
        rows = slice(blk * c, (blk + 1) * c) if isinstance(blk, int) else pl.ds(pl.multiple_of(blk * c, c), c)
        cols = slice(h * d, (h + 1) * d)

        def emit(hh, C_new, n_new, m_new):
            y_ref[rows, cols] = (jax.nn.sigmoid(ob_ref[rows, cols]) * hh).astype(y_ref.dtype)
            C_ref[i, h] = C_new
            n_ref[i, h:h + 1, :] = n_new
            m_ref[i, :, h:h + 1] = m_new

        if cps > 1:
            qh, kh, vh = (r[rows, cols].astype(F32) for r in (q_ref, k_ref, v_ref))
            gt = g_ref[rows, :]
        else:
            qh, kh, vh, gt = q[rows, cols], k[rows, cols], v[rows, cols], gates[rows, :]
        yield from _mlstm_head(qh, kh, vh, gt[:, h:h + 1], gt[:, H_B + h:H_B + h + 1], C_ref[i, h],
                               n_ref[i, h:h + 1, :], m_ref[i, :, h:h + 1], emit)

    if cps > 1:
        @pl.loop(0, cps)
        def _(blk):
            for h in range(H_B):
                _run_staged([chain(0, h, blk)])
    else:
        pairs = [(i, h) for i in range(bb) for h in range(H_B)]
        group = MLSTM_GROUP_SHORT if c <= 64 else 1
        for c0 in range(0, len(pairs), group):
            _run_staged([chain(i, h, i) for i, h in pairs[c0:c0 + group]])


def mlstm(q, k, v, ob, gates, C0, n0, m0, N, L, *, bb, out_dtype, cps=4):
    c = MLSTM_CHUNK if L % MLSTM_CHUNK == 0 else L
    nc = L // c
    d = DK_B
    assert bb == 1 or nc == 1
    cps = math.gcd(cps, nc) if bb == 1 else 1
    nc //= cps
    rows = lambda width: pl.BlockSpec((bb * cps * c, width), lambda n, j: (n * nc + j, 0))
    st_C = pl.BlockSpec((bb, H_B, d, d), lambda n, j: (n, 0, 0, 0))
    st_n = pl.BlockSpec((bb, H_B, d), lambda n, j: (n, 0, 0))
    st_m = pl.BlockSpec((bb, 1, H_B), lambda n, j: (n, 0, 0))
    y, C, n, m = pl.pallas_call(
        functools.partial(_mlstm_kernel, bb=bb, c=c, cps=cps),
        grid=(N // bb, nc),
        in_specs=[rows(H_B * d), rows(H_B * d), rows(H_B * d), rows(H_B * d), rows(gates.shape[1]), st_C, st_n, st_m],
        out_specs=[rows(H_B * d), st_C, st_n, st_m],
        out_shape=[jax.ShapeDtypeStruct((N * L, H_B * d), out_dtype), jax.ShapeDtypeStruct((N, H_B, d, d), F32),
                   jax.ShapeDtypeStruct((N, H_B, d), F32), jax.ShapeDtypeStruct((N, 1, H_B), F32)],
        compiler_params=_cparams("parallel", "arbitrary"),
    )(q, k, v, ob, gates, C0, n0, m0.reshape(N, 1, H_B))
    return y, C, n, m.reshape(N, H_B)


def _diff_lambda(lq1_ref, lk1_ref, lq2_ref, lk2_ref, lam_init):
    return (jnp.exp(jnp.sum(lq1_ref[...] * lk1_ref[...], axis=-1, keepdims=True))
            - jnp.exp(jnp.sum(lq2_ref[...] * lk2_ref[...], axis=-1, keepdims=True)) + lam_init)


def _diff_head_norm(o, gain, lam_init):
    return _rms(o, gain) * (1.0 - lam_init)


def diff_key_extension(n_groups):
    width = 2 * DH_C
    lane = jnp.arange(n_groups * width) % width
    one = lambda sel: jnp.where(sel, 1.0, 0.0).astype(F32).reshape(1, -1)
    return one((lane == DH_C) | (lane == DH_C + 1)), one(lane == DH_C + 2), one(lane == DH_C + 3)


def _diff_prompt_kernel(qi_tab, ki_tab, slope_ref, q_ref, k_ref, vt_ref, lq1_ref, lk1_ref, lq2_ref, lk2_ref, gain_ref,
                        o_ref, qx_s, m_s, l_s, acc_s, *, tq, tk, lam_init):
    kvh, step = pl.program_id(1), pl.program_id(2)
    qi, ki = qi_tab[step], ki_tab[step]
    dv = 2 * DH_C

    @pl.when(ki == 0)
    def _():
        m_s[...] = jnp.full(m_s.shape, -jnp.inf, F32)
        l_s[...] = jnp.zeros(l_s.shape, F32)
        acc_s[...] = jnp.zeros(acc_s.shape, F32)
        pos_hi, pos_lo = _split_pos(qi * tq + lax.broadcasted_iota(jnp.int32, (tq, DH_C), 0))
        lane = lax.broadcasted_iota(jnp.int32, (tq, DH_C), 1)
        for g in range(G_C):
            slope = slope_ref[0, kvh * G_C + g]
            ext = jnp.where(lane == 0, -slope * pos_hi,
                            jnp.where(lane == 1, -slope * pos_lo, jnp.where(lane <= 3, slope, 0.0))).astype(BF16)
            for mm in range(2):
                qh = q_ref[:, g * dv + mm * DH_C:g * dv + (mm + 1) * DH_C]
                qx_s[g * 2 + mm] = jnp.concatenate([qh, ext], axis=1)

    def block(k0, nk_, triangular):
        k = k_ref[k0:k0 + nk_, :]
        vt = vt_ref[:, k0:k0 + nk_]
        if triangular:
            keep = (lax.broadcasted_iota(jnp.int32, (nk_, tq), 1) >= lax.broadcasted_iota(jnp.int32, (nk_, tq), 0))
        n_heads = 2 * G_C

        def scores(idx):
            mm = idx % 2
            return _dot_nt(k[:, mm * dv:(mm + 1) * dv], qx_s[idx])

        nxt = scores(0)
        for idx in range(n_heads):
            logits = nxt
            if idx + 1 < n_heads:
                nxt = scores(idx + 1)
            if triangular:
                logits = jnp.where(keep, logits, -jnp.inf)
            m_old = m_s[idx]
            m_new = jnp.maximum(m_old, jnp.max(logits, axis=0, keepdims=True))
            alpha = jnp.exp(m_old - m_new)
            p = jnp.exp(logits - m_new)
            l_s[idx] = alpha * l_s[idx] + jnp.sum(p, axis=0, keepdims=True)
            acc_s[idx] = alpha * acc_s[idx] + _dot(vt, p.astype(BF16))
            m_s[idx] = m_new

    n_sub = tk // tq
    diag_sub = qi % n_sub
    on_diagonal = ki == (qi * tq) // tk
    pl.when(jnp.logical_not(on_diagonal))(lambda: block(0, tk, False))
    for j in range(n_sub):
        if j + 1 < n_sub:
            pl.when(on_diagonal & (j < diag_sub))(functools.partial(block, j * tq, tq, False))
        pl.when(on_diagonal & (j == diag_sub))(functools.partial(block, j * tq, tq, True))

    @pl.when(on_diagonal)
    def _():
        lam = _diff_lambda(lq1_ref, lk1_ref, lq2_ref, lk2_ref, lam_init)
        for g in range(G_C):
            ot = acc_s[2 * g] / l_s[2 * g] - lam * (acc_s[2 * g + 1] / l_s[2 * g + 1])
            ot = ot * lax.rsqrt(jnp.mean(ot * ot, axis=0, keepdims=True) + EPS) * gain_ref[...] * (1.0 - lam_init)
            o_ref[:, g * dv:(g + 1) * dv] = ot.T.astype(o_ref.dtype)


def diff_prompt(q, kx, vt, lams, gain, lam_init, B, S, *, tq=512, tk=1024):
    dv = 2 * DH_C
    tk = min(tk, S)
    tq = min(tq, tk)
    assert tk % tq == 0 and S % tk == 0 and math.log2(H_C).is_integer() and H_C <= 8
    nq, nk = S // tq, S // tk
    pairs = [(qi, ki) for qi in range(nq) for ki in range((qi * tq + tq - 1) // tk + 1)]
    qi_tab = jnp.asarray([p[0] for p in pairs], jnp.int32)
    ki_tab = jnp.asarray([p[1] for p in pairs], jnp.int32)
    vec = lambda a: a.reshape(1, -1)
    const = lambda w: pl.BlockSpec((1, w), lambda b, h, s, qt, kt: (0, 0))
    grid_spec = pltpu.PrefetchScalarGridSpec(
        num_scalar_prefetch=2,
        grid=(B, KVH_C, len(pairs)),
        in_specs=[pl.BlockSpec(memory_space=pltpu.SMEM),
                  pl.BlockSpec((tq, G_C * dv), lambda b, h, s, qt, kt: (b * nq + qt[s], h)),
                  pl.BlockSpec((tk, 2 * dv), lambda b, h, s, qt, kt: (b * nk + kt[s], h)),
                  pl.BlockSpec((None, dv, tk), lambda b, h, s, qt, kt: (b, h, kt[s])),
                  const(DH_C), const(DH_C), const(DH_C), const(DH_C),
                  pl.BlockSpec((dv, 1), lambda b, h, s, qt, kt: (0, 0))],
        out_specs=pl.BlockSpec((tq, G_C * dv), lambda b, h, s, qt, kt: (b * nq + qt[s], h)),
        scratch_shapes=[pltpu.VMEM((2 * G_C, tq, dv), BF16), pltpu.VMEM((2 * G_C, 1, tq), F32),
                        pltpu.VMEM((2 * G_C, 1, tq), F32), pltpu.VMEM((2 * G_C, dv, tq), F32)],
    )
    return pl.pallas_call(
        functools.partial(_diff_prompt_kernel, tq=tq, tk=tk, lam_init=lam_init),
        grid_spec=grid_spec,
        out_shape=jax.ShapeDtypeStruct(q.shape, BF16),
        compiler_params=_cparams("parallel", "parallel", "arbitrary"),
    )(qi_tab, ki_tab, jnp.asarray([[_alibi_slope(h, H_C) for h in range(H_C)]], F32), q, kx, vt,
      *[vec(a) for a in lams], gain.reshape(dv, 1))


def _diff_sample_kernel(pt_ref, q_ref, kn_ref, vn_ref, lq1_ref, lk1_ref, lq2_ref, lk2_ref, gain_ref, kt_hbm, v_hbm,
                        o_ref, kbuf, vbuf, sems, qbd_s, m_s, l_s, acc_s, *, pp, T, past, lam_init):
    b, j = pl.program_id(0), pl.program_id(1)
    nb, nj = pl.num_programs(0), pl.num_programs(1)
    dv = 2 * DH_C

    n_slots = kbuf.shape[0]
    ahead = n_slots - 1
    step = b * nj + j
    slot = step % n_slots

    def page_copies(s, sl, read_table=True):
        copies = []
        for p in range(pp):
            page = pt_ref[s // nj, (s % nj) * pp + p] if read_table else 0
            copies.append(pltpu.make_async_copy(kt_hbm.at[page], kbuf.at[sl, p], sems.at[0, sl]))
            copies.append(pltpu.make_async_copy(v_hbm.at[page], vbuf.at[sl, p], sems.at[1, sl]))
        return copies

    @pl.when(step == 0)
    def _():
        for s in range(ahead):
            for cp in page_copies(s, s):
                cp.start()

    @pl.when(step + ahead < nb * nj)
    def _():
        for cp in page_copies(step + ahead, (step + ahead) % n_slots):
            cp.start()

    for cp in page_copies(step, slot, read_table=False):
        cp.wait()
    k_refs = [kbuf.at[slot, p] for p in range(pp)]
    v_refs = [vbuf.at[slot, p] for p in range(pp)]
    RK = 2 * G_C * T
    R = KVH_C * RK
    row = lax.broadcasted_iota(jnp.int32, (R, 1), 0)
    t_row = row % T
    head_row = (row // RK) * G_C + (row // T) % G_C
    slope = jnp.exp2(-8.0 * (head_row + 1).astype(F32) / H_C)

    @pl.when(j == 0)
    def _():
        qbd_s[...] = jnp.zeros(qbd_s.shape, F32)
        for kvh in range(KVH_C):
            for mm in range(2):
                for g in range(G_C):
                    r0 = kvh * RK + (mm * G_C + g) * T
                    c0 = (kvh * 2 + mm) * DH_C
                    q0 = ((kvh * G_C + g) * 2 + mm) * DH_C
                    qbd_s[r0:r0 + T, c0:c0 + DH_C] = q_ref[:, q0:q0 + DH_C]
        m_s[...] = jnp.full(m_s.shape, -jnp.inf, F32)
        l_s[...] = jnp.zeros(l_s.shape, F32)
        acc_s[...] = jnp.zeros(acc_s.shape, F32)

    qbd = qbd_s[...].astype(BF16)

    def update(scores, values, dist, keep):
        logits = scores - slope * dist.astype(F32)
        if keep is not None:
            logits = jnp.where(keep, logits, -jnp.inf)
        m_old = m_s[...]
        m_new = jnp.maximum(m_old, jnp.max(logits, axis=-1, keepdims=True))
        alpha = jnp.exp(m_old - m_new)
        p = jnp.exp(logits - m_new)
        l_s[...] = alpha * l_s[...] + jnp.sum(p, axis=-1, keepdims=True)
        m_s[...] = m_new
        p = p.astype(BF16)
        for kvh in range(KVH_C):
            rows = slice(kvh * RK, (kvh + 1) * RK)
            acc_s[rows, :] = alpha[rows, :] * acc_s[rows, :] + _dot(p[rows, :], values(kvh))

    kb = jnp.concatenate([r[...].astype(BF16) for r in k_refs], axis=1)
    page_values = lambda kvh: jnp.concatenate(
        [r[pl.ds(kvh, PAGE, stride=KVH_C), :].astype(BF16) for r in v_refs], axis=0)
    kpos = j * (pp * PAGE) + lax.broadcasted_iota(jnp.int32, (R, pp * PAGE), 1)
    update(_dot(qbd, kb), page_values, past + t_row - kpos, None)

    @pl.when(j == nj - 1)
    def _():
        pad = jnp.zeros((PAGE - T, kn_ref.shape[1]), F32)
        kn = jnp.concatenate([kn_ref[...], pad], axis=0).astype(BF16)
        vn = jnp.concatenate([vn_ref[...], pad], axis=0).astype(BF16)
        s_new = lax.broadcasted_iota(jnp.int32, (R, PAGE), 1)
        update(_dot_nt(qbd, kn), lambda kvh: vn[:, kvh * dv:(kvh + 1) * dv], t_row - s_new, s_new <= t_row)
        lam = _diff_lambda(lq1_ref, lk1_ref, lq2_ref, lk2_ref, lam_init)
        on = acc_s[...] / l_s[...]
        half = G_C * T
        for kvh in range(KVH_C):
            o = on[kvh * RK:kvh * RK + half, :] - lam * on[kvh * RK + half:(kvh + 1) * RK, :]
            o = _diff_head_norm(o, gain_ref[...], lam_init)
            for g in range(G_C):
                h = kvh * G_C + g
                o_ref[:, h * dv:(h + 1) * dv] = o[g * T:(g + 1) * T, :]


def diff_sample(q, kn, vn, cache_kt, cache_v, page_table, lams, gain, lam_init, N, T, *, pp=16, n_slots=3):
    dv = 2 * DH_C
    n_pages = page_table.shape[1]
    pp = math.gcd(pp, n_pages)
    R = KVH_C * 2 * G_C * T
    width = KVH_C * 2 * DH_C
    assert cache_kt.shape[1:] == (width, PAGE) and cache_v.shape[1:] == (PAGE * KVH_C, dv)
    assert N * (n_pages // pp) >= n_slots - 1
    vec = lambda a: a.reshape(1, -1)
    rows = lambda w: pl.BlockSpec((T, w), lambda b, j, pt: (b, 0))
    const = lambda w: pl.BlockSpec((1, w), lambda b, j, pt: (0, 0))
    in_hbm = pl.BlockSpec(memory_space=pl.ANY)
    grid_spec = pltpu.PrefetchScalarGridSpec(
        num_scalar_prefetch=1,
        grid=(N, n_pages // pp),
        in_specs=[rows(q.shape[1]), rows(width), rows(width), const(DH_C), const(DH_C), const(DH_C), const(DH_C),
                  const(dv), in_hbm, in_hbm],
        out_specs=rows(q.shape[1]),
        scratch_shapes=[pltpu.VMEM((n_slots, pp, width, PAGE), F32), pltpu.VMEM((n_slots, pp, PAGE * KVH_C, dv), F32),
                        pltpu.SemaphoreType.DMA((2, n_slots)),
                        pltpu.VMEM((R, width), F32), pltpu.VMEM((R, 1), F32), pltpu.VMEM((R, 1), F32),
                        pltpu.VMEM((R, dv), F32)],
    )
    return pl.pallas_call(
        functools.partial(_diff_sample_kernel, pp=pp, T=T, past=n_pages * PAGE, lam_init=lam_init),
        grid_spec=grid_spec,
        out_shape=jax.ShapeDtypeStruct(q.shape, F32),
        compiler_params=_cparams("arbitrary", "arbitrary"),
    )(page_table, q, kn, vn, *[vec(a) for a in lams], vec(gain), cache_kt, cache_v)


LANES = 128


def _cross_kernel(q_ref, mk_ref, mv_ref, o_ref, *, bb, tq, interleaved):
    D = q_ref.shape[1]
    nch = D // XH // LANES
    if interleaved:
        M = mk_ref.shape[1] // (XH * nch)
        chunk = lambda ref, i, h, c: ref[i, pl.ds(c * XH + h, M, stride=XH * nch), :]
    else:
        chunk = lambda ref, i, h, c: ref[i, :, (h * nch + c) * LANES:(h * nch + c + 1) * LANES]
    col = lambda h, c: slice((h * nch + c) * LANES, (h * nch + c + 1) * LANES)
    rows = lambda i: slice(i * tq, (i + 1) * tq)

    def scores(i, h):
        q = q_ref[rows(i), h * nch * LANES:(h + 1) * nch * LANES].astype(BF16)
        return sum(_dot_nt(q[:, c * LANES:(c + 1) * LANES], chunk(mk_ref, i, h, c).astype(BF16)) for c in range(nch))

    def probs(s):
        e = jnp.exp(s - jnp.max(s, axis=-1, keepdims=True))
        return (e * (1.0 / jnp.sum(e, axis=-1, keepdims=True))).astype(BF16)

    def output(i, h, p):
        for c in range(nch):
            o_ref[rows(i), col(h, c)] = _dot(p, chunk(mv_ref, i, h, c).astype(BF16)).astype(o_ref.dtype)

    pairs = [(i, h) for i in range(bb) for h in range(XH)]
    group = 2 * XH if tq <= 64 else 1
    for g0 in range(0, len(pairs), group):
        grp = pairs[g0:g0 + group]
        ss = [scores(i, h) for i, h in grp]
        ps = [probs(s) for s in ss]
        for (i, h), p in zip(grp, ps):
            output(i, h, p)


def interleave_heads(mem):
    depth, N, M, XH_, xdh = mem.shape
    nch = xdh // LANES
    return jnp.transpose(mem.reshape(depth, N, M, XH_, nch, LANES), (0, 1, 2, 4, 3, 5)).reshape(
        depth, N, M * XH_ * nch, LANES)


def cross_core(q, mk, mv, N, L, *, bb, tq, out_dtype, layer=None):
    D = q.shape[1]
    nt = L // tq
    assert bb == 1 or nt == 1
    if layer is None:
        mem = pl.BlockSpec((bb,) + mk.shape[1:], lambda n, i: (n, 0, 0))
    else:
        mem = pl.BlockSpec((None, bb) + mk.shape[2:], lambda n, i: (layer, n, 0, 0))
    return pl.pallas_call(
        functools.partial(_cross_kernel, bb=bb, tq=tq, interleaved=layer is not None),
        grid=(N // bb, nt),
        in_specs=[pl.BlockSpec((bb * tq, D), lambda n, i: (n * nt + i, 0)), mem, mem],
        out_specs=pl.BlockSpec((bb * tq, D), lambda n, i: (n * nt + i, 0)),
        out_shape=jax.ShapeDtypeStruct(q.shape, out_dtype),
        compiler_params=_cparams("parallel", "parallel"),
    )(q, mk, mv)


def _lambda_init(layer):
    return 0.8 - 0.6 * math.exp(-0.3 * layer)


def _ab_segments(act_dtype):
    qa, kv, hb = H_A * DH_A, 2 * KVH_A * DH_A, H_B * DK_B
    o = 0
    segs = []
    for width, dtype, scale in ((qa, act_dtype, DH_A ** -0.5), (kv, F32, 1.0), (hb, act_dtype, 1.0),
                                (hb, act_dtype, DK_B ** -0.5), (hb, act_dtype, 1.0), (hb, F32, 1.0)):
        segs.append(Seg(o, width, dtype, scale))
        o += width
    segs.append(Seg(o, 128, F32, 1.0, True))
    return segs, o + 128


def kernel(x_prompt, x_sample, mem_prompt, cache_swa_k, cache_swa_v, state_mlstm_C, state_mlstm_n, state_mlstm_m, cache_diff_k, cache_diff_v, page_table, cache_mem_k, cache_mem_v, norm_mix, norm_cross, norm_ffn, norm_final, w_in_ab, b_mlstm_i, b_mlstm_f, attn_sinks, w_out_ab, w_in_c, lambda_q1, lambda_k1, lambda_q2, lambda_k2, diff_norm, w_out_c, w_xq, w_xk, w_xv, w_xo, w_gate, w_up, w_down):
    B, S, D = x_prompt.shape
    NS, TS, _ = x_sample.shape
    M = mem_prompt.shape[1]
    depth = norm_mix.shape[0]
    TMP = 512
    TMS = min(512, NS * TS)
    xdh = D // XH
    bf = lambda a: a.astype(BF16)

    xp = x_prompt.reshape(B * S, D)
    xs = x_sample.reshape(NS * TS, D)
    mem = mem_prompt.reshape(B * M, D)
    zero_bias = lambda n: jnp.zeros((1, n), F32)

    outs = {k: [] for k in ("swa_kp", "swa_vp", "swa_ks", "swa_vs", "C_p", "n_p", "m_p", "C_s", "n_s", "m_s",
                            "dk_p", "dv_p", "dk_s", "dv_s", "mk_p", "mv_p")}
    for layer in range(depth):
        if layer % 2 == 0:
            e = layer // 2
            segs_p, n_cols = _ab_segments(BF16)
            segs_s, _ = _ab_segments(F32)
            w_in = bf(jnp.pad(w_in_ab[e], ((0, 0), (0, n_cols - w_in_ab.shape[2]))))
            bias = jnp.zeros((1, n_cols), F32).at[0, n_cols - 128:n_cols - 128 + 2 * H_B].set(
                jnp.concatenate([b_mlstm_i[e], b_mlstm_f[e]]))
            nqa = H_A * DH_A
            w_out_a, w_out_b = bf(w_out_ab[e, :nqa]), bf(w_out_ab[e, nqa:])
            nk = KVH_A * DH_A

            qa, kva, qb, kb, vb, ob, gates = norm_matmul(xp, norm_mix[layer], w_in, bias, segs_p, tm=TMP)
            ya = swa_prompt(qa, kva, attn_sinks[e], B, S)
            yb, C, n, m = mlstm(qb, kb, vb, ob, gates, jnp.zeros((B, H_B, DK_B, DK_B), F32),
                                jnp.zeros((B, H_B, DK_B), F32), jnp.zeros((B, H_B), F32), B, S, bb=1, out_dtype=BF16)
            mix_p = ([ya, yb], [w_out_a, w_out_b])
            kv_last = kva.reshape(B, S, 2 * nk)[:, S - WINDOW:]
            outs["swa_kp"].append(kv_last[..., :nk].reshape(B, WINDOW, KVH_A, DH_A))
            outs["swa_vp"].append(kv_last[..., nk:].reshape(B, WINDOW, KVH_A, DH_A))
            outs["C_p"].append(C); outs["n_p"].append(n); outs["m_p"].append(m)

            qa, kva, qb, kb, vb, ob, gates = norm_matmul(xs, norm_mix[layer], w_in, bias, segs_s, tm=TMS)
            ya, k_new, v_new = swa_sample(qa, kva, cache_swa_k[e].reshape(NS, WINDOW, nk),
                                          cache_swa_v[e].reshape(NS, WINDOW, nk), attn_sinks[e], NS, TS)
            yb, C, n, m = mlstm(qb, kb, vb, ob, gates, state_mlstm_C[e], state_mlstm_n[e], state_mlstm_m[e], NS, TS,
                                bb=8, out_dtype=F32)
            xs = matmul_residual([ya, yb], [w_out_a, w_out_b], xs, tm=TMS)
            outs["swa_ks"].append(k_new.reshape(NS, WINDOW, KVH_A, DH_A))
            outs["swa_vs"].append(v_new.reshape(NS, WINDOW, KVH_A, DH_A))
            outs["C_s"].append(C); outs["n_s"].append(n); outs["m_s"].append(m)
        else:
            c = layer // 2
            lam_init = _lambda_init(layer)
            lams = (lambda_q1[c], lambda_k1[c], lambda_q2[c], lambda_k2[c])
            nq, nkv = H_C * 2 * DH_C, KVH_C * 2 * DH_C
            w_in = bf(w_in_c[c])
            w_out = bf(w_out_c[c])

            n_groups = KVH_C * 2
            w_kx = jnp.pad(w_in[:, nq:nq + nkv].reshape(D, n_groups, DH_C), ((0, 0), (0, 0), (0, DH_C)))
            w_px = jnp.concatenate([w_in[:, :nq], w_kx.reshape(D, 2 * nkv), w_in[:, nq + nkv:]], axis=1)
            ones, u_hi, u_lo = diff_key_extension(n_groups)
            widen = lambda a: jnp.pad(a, ((0, 0), (nq, nkv)))
            o_kx, o_v = nq, nq + 2 * nkv
            segs = [Seg(0, nq, BF16, DH_C ** -0.5), Seg(o_kx, 2 * nkv, BF16, 1.0, True, True), Seg(o_v, nkv, F32)]
            q, kx, v, kt, vt = norm_matmul(
                xp, norm_mix[layer], w_px, widen(ones), segs, tm=TMP, pos=(widen(u_hi), widen(u_lo), S),
                w_t=[(w_in[:, nq:nq + nkv].T, F32), (w_in[:, nq + nkv:].T, BF16)])
            o = diff_prompt(q, kx, vt, lams, diff_norm[c], lam_init, B, S)
            mix_p = ([o], [w_out])
            outs["dk_p"].append(jnp.transpose(kt.reshape(B, KVH_C, 2, DH_C, S), (0, 4, 1, 2, 3)))
            outs["dv_p"].append(v.reshape(B, S, KVH_C, 2 * DH_C))

            segs = [Seg(0, nq, F32, DH_C ** -0.5), Seg(nq, nkv, F32), Seg(nq + nkv, nkv, F32)]
            q, k, v = norm_matmul(xs, norm_mix[layer], w_in, zero_bias(nq + 2 * nkv), segs, tm=TMS)
            n_pool = cache_diff_k.shape[1]
            cache_kt = jnp.transpose(cache_diff_k[c], (0, 2, 3, 4, 1)).reshape(n_pool, nkv, PAGE)
            cache_vr = cache_diff_v[c].reshape(n_pool, PAGE * KVH_C, 2 * DH_C)
            o = diff_sample(q, k, v, cache_kt, cache_vr, page_table, lams, diff_norm[c], lam_init, NS, TS)
            xs = matmul_residual([o], [w_out], xs, tm=TMS)
            outs["dk_s"].append(k.reshape(NS, TS, KVH_C, 2, DH_C))
            outs["dv_s"].append(v.reshape(NS, TS, KVH_C, 2 * DH_C))

        w_kv = bf(jnp.concatenate([w_xk[layer], w_xv[layer]], axis=1))
        segs = [Seg(0, D, F32), Seg(D, D, F32), Seg(0, D, BF16), Seg(D, D, BF16)]
        mk, mv, mk16, mv16 = norm_matmul(mem, jnp.ones((D,), F32), w_kv, zero_bias(2 * D), segs, tm=TMP, norm=False)
        outs["mk_p"].append(mk.reshape(B, M, XH, xdh))
        outs["mv_p"].append(mv.reshape(B, M, XH, xdh))
        w_q, w_o = bf(w_xq[layer]), bf(w_xo[layer])
        last = layer == depth - 1
        wg, wu, wd = bf(w_gate[layer]), bf(w_up[layer]), bf(w_down[layer])

        xp = post_mixer(*mix_p, xp, norm_cross[layer], w_q, mk16.reshape(B, M, D), mv16.reshape(B, M, D), w_o,
                        norm_ffn[layer], wg, wu, wd, norm_final, B, S, tq=TMP, final_norm=last)

        (q,) = norm_matmul(xs, norm_cross[layer], w_q, zero_bias(D), [Seg(0, D, F32, xdh ** -0.5)], tm=TMS)
        o = cross_core(q, interleave_heads(cache_mem_k), interleave_heads(cache_mem_v), NS, TS, bb=8, tq=TS,
                       out_dtype=F32, layer=layer)
        xs = matmul_residual([o], [w_o], xs, tm=TMS)
        xs = ffn(xs, norm_ffn[layer], wg, wu, wd, norm_final, tm=TMS, final_norm=last)

    st = jnp.stack
    return (xp.reshape(B, S, D), xs.reshape(NS, TS, D),
            st(outs["swa_kp"]), st(outs["swa_vp"]), st(outs["swa_ks"]), st(outs["swa_vs"]),
            st(outs["C_p"]), st(outs["n_p"]), st(outs["m_p"]), st(outs["C_s"]), st(outs["n_s"]), st(outs["m_s"]),
            st(outs["dk_p"]), st(outs["dv_p"]), st(outs["dk_s"]), st(outs["dv_s"]),
            st(outs["mk_p"]), st(outs["mv_p"]))
```
